```python
import jax, jax.numpy as jnp
from jax import lax
import numpy as np

D_MODEL = 1024
BATCH = 1
SEQ = 16384
DEPTH = 2

N_A_LAYERS = DEPTH // 2
N_B_LAYERS = DEPTH - N_A_LAYERS
MLA_HEADS = 8
QK_NOPE = 128
QK_ROPE = 64
V_HEAD = 128
Q_LORA = 512
KV_LORA = 256
ROPE_THETA = 10000.0
DIFF_HEADS = 8
DIFF_HEAD = 64
DIFF_V = 2 * DIFF_HEAD
LAMBDA_STD = 0.1
D_FF = ((8 * D_MODEL // 3 + 255) // 256) * 256
EPS = 1e-6
Q_BLOCK = 128
NEG_INF = -1e30

kernel_name = "yoco_mla_diffattn_alibi_swiglu"


def _rmsnorm(x, g):
    xf = x.astype(jnp.float32)
    r = lax.rsqrt(jnp.mean(xf * xf, axis=-1, keepdims=True) + EPS)
    return (xf * r).astype(x.dtype) * g


def _rope(x, pos):
    half = QK_ROPE // 2
    inv = ROPE_THETA ** (-jnp.arange(half, dtype=jnp.float32) * 2.0 / QK_ROPE)
    ang = pos.astype(jnp.float32)[..., None] * inv
    cos = jnp.cos(ang)[:, :, None, :]
    sin = jnp.sin(ang)[:, :, None, :]
    x1 = x[..., :half].astype(jnp.float32)
    x2 = x[..., half:].astype(jnp.float32)
    return jnp.concatenate([x1 * cos - x2 * sin, x2 * cos + x1 * sin], axis=-1).astype(x.dtype)


def _alibi_slopes(n_heads):
    return 2.0 ** (-8.0 * jnp.arange(1, n_heads + 1, dtype=jnp.float32) / n_heads)


def _sweep(block_fn, *q_arrays):
    b, s = q_arrays[0].shape[:2]
    nb = s // Q_BLOCK

    def to_blocks(a):
        return jnp.moveaxis(a.reshape((b, nb, Q_BLOCK) + a.shape[2:]), 1, 0)

    starts = jnp.arange(nb, dtype=jnp.int32) * Q_BLOCK
    out = lax.map(lambda args: block_fn(*args), (starts,) + tuple(to_blocks(a) for a in q_arrays))
    return jnp.moveaxis(out, 0, 1).reshape((b, s) + out.shape[3:])


def _mla(h, pos, w_dq, q_norm, w_uq, w_dkv, kv_norm, w_ukv, w_o):
    b, s, _ = h.shape
    cq = _rmsnorm(h @ w_dq, q_norm)
    q = (cq @ w_uq).reshape(b, s, MLA_HEADS, QK_NOPE + QK_ROPE)
    q = jnp.concatenate([q[..., :QK_NOPE], _rope(q[..., QK_NOPE:], pos)], axis=-1)
    ckv = h @ w_dkv
    c_kv = _rmsnorm(ckv[..., :KV_LORA], kv_norm)
    k_pe = _rope(ckv[..., None, KV_LORA:], pos)
    kv = (c_kv @ w_ukv).reshape(b, s, MLA_HEADS, QK_NOPE + V_HEAD)
    k = jnp.concatenate([kv[..., :QK_NOPE],
                         jnp.broadcast_to(k_pe, (b, s, MLA_HEADS, QK_ROPE))], axis=-1)
    v = kv[..., QK_NOPE:]
    scale = (QK_NOPE + QK_ROPE) ** -0.5
    key_idx = jnp.arange(s, dtype=jnp.int32)

    def block(start, qb):
        sc = jnp.einsum('bqhd,bkhd->bhqk', qb, k).astype(jnp.float32) * scale
        mask = (start + jnp.arange(Q_BLOCK, dtype=jnp.int32))[:, None] >= key_idx[None, :]
        p = jax.nn.softmax(jnp.where(mask, sc, NEG_INF), axis=-1).astype(v.dtype)
        return jnp.einsum('bhqk,bkhd->bqhd', p, v)

    o = _sweep(block, q)
    return o.reshape(b, s, MLA_HEADS * V_HEAD) @ w_o


def _shared_kv(h, kv_norm, w_k, w_v):
    b, s, _ = h.shape
    hk = _rmsnorm(h, kv_norm)
    k = (hk @ w_k).reshape(b, s, DIFF_HEADS, 2, DIFF_HEAD)
    v = (hk @ w_v).reshape(b, s, DIFF_HEADS, DIFF_V)
    return k[..., 0, :], k[..., 1, :], v


def _diff_attn(h, pos, k1, k2, v, w_q, lq1, lk1, lq2, lk2, subln, w_o, lambda_init):
    b, s, _ = h.shape
    q = (h @ w_q).reshape(b, s, DIFF_HEADS, 2, DIFF_HEAD)
    q1, q2 = q[..., 0, :], q[..., 1, :]
    lam = (jnp.exp(jnp.sum(lq1.astype(jnp.float32) * lk1.astype(jnp.float32)))
           - jnp.exp(jnp.sum(lq2.astype(jnp.float32) * lk2.astype(jnp.float32)))
           + lambda_init)
    slopes = _alibi_slopes(DIFF_HEADS)
    scale = DIFF_HEAD ** -0.5
    key_idx = jnp.arange(s, dtype=jnp.int32)

    def block(start, q1b, q2b, posb):
        dist = (posb[:, :, None] - pos[:, None, :]).astype(jnp.float32)
        bias = -slopes[None, :, None, None] * dist[:, None]
        mask = (start + jnp.arange(Q_BLOCK, dtype=jnp.int32))[:, None] >= key_idx[None, :]

        def probs(qb, kk):
            sc = jnp.einsum('bqhd,bkhd->bhqk', qb, kk).astype(jnp.float32) * scale + bias
            return jax.nn.softmax(jnp.where(mask, sc, NEG_INF), axis=-1)

        a = probs(q1b, k1) - lam * probs(q2b, k2)
        return jnp.einsum('bhqk,bkhd->bqhd', a.astype(v.dtype), v)

    o = _sweep(block, q1, q2, pos)
    o = _rmsnorm(o, subln) * (1.0 - lambda_init)
    return o.reshape(b, s, DIFF_HEADS * DIFF_V) @ w_o


def _swiglu(h, w_gate_up, w_down):
    gu = h @ w_gate_up
    return (jax.nn.silu(gu[..., :D_FF]) * gu[..., D_FF:]) @ w_down


def setup_inputs(seed: int = 0) -> dict:
    key = jax.random.key(seed)
    ks = jax.random.split(key, 32)

    def w(k, shape, fan_in):
        return jax.random.normal(k, shape, jnp.float32) * (fan_in ** -0.5)

    def gain(k, shape):
        return 1.0 + 0.02 * jax.random.normal(k, shape, jnp.float32)

    nA, nB = N_A_LAYERS, N_B_LAYERS
    return {
        "x": jax.random.normal(ks[0], (BATCH, SEQ, D_MODEL), jnp.float32),
        "positions": jnp.broadcast_to(jnp.arange(SEQ, dtype=jnp.int32)[None, :], (BATCH, SEQ)),
        "attn_norm": gain(ks[1], (DEPTH, D_MODEL)),
        "ffn_norm": gain(ks[2], (DEPTH, D_MODEL)),
        "final_norm": gain(ks[3], (D_MODEL,)),
        "mla_w_dq": w(ks[4], (nA, D_MODEL, Q_LORA), D_MODEL),
        "mla_q_norm": gain(ks[5], (nA, Q_LORA)),
        "mla_w_uq": w(ks[6], (nA, Q_LORA, MLA_HEADS * (QK_NOPE + QK_ROPE)), Q_LORA),
        "mla_w_dkv": w(ks[7], (nA, D_MODEL, KV_LORA + QK_ROPE), D_MODEL),
        "mla_kv_norm": gain(ks[8], (nA, KV_LORA)),
        "mla_w_ukv": w(ks[9], (nA, KV_LORA, MLA_HEADS * (QK_NOPE + V_HEAD)), KV_LORA),
        "mla_w_o": w(ks[10], (nA, MLA_HEADS * V_HEAD, D_MODEL), MLA_HEADS * V_HEAD),
        "diff_kv_norm": gain(ks[11], (D_MODEL,)),
        "diff_w_k": w(ks[12], (D_MODEL, DIFF_HEADS * 2 * DIFF_HEAD), D_MODEL),
        "diff_w_v": w(ks[13], (D_MODEL, DIFF_HEADS * DIFF_V), D_MODEL),
        "diff_w_q": w(ks[14], (nB, D_MODEL, DIFF_HEADS * 2 * DIFF_HEAD), D_MODEL),
        "diff_lambda_q1": LAMBDA_STD * jax.random.normal(ks[15], (nB, DIFF_HEAD), jnp.float32),
        "diff_lambda_k1": LAMBDA_STD * jax.random.normal(ks[16], (nB, DIFF_HEAD), jnp.float32),
        "diff_lambda_q2": LAMBDA_STD * jax.random.normal(ks[17], (nB, DIFF_HEAD), jnp.float32),
        "diff_lambda_k2": LAMBDA_STD * jax.random.normal(ks[18], (nB, DIFF_HEAD), jnp.float32),
        "diff_subln": gain(ks[19], (nB, DIFF_V)),
        "diff_w_o": w(ks[20], (nB, DIFF_HEADS * DIFF_V, D_MODEL), DIFF_HEADS * DIFF_V),
        "ffn_w_gate_up": w(ks[21], (DEPTH, D_MODEL, 2 * D_FF), D_MODEL),
        "ffn_w_down": w(ks[22], (DEPTH, D_FF, D_MODEL), D_FF),
    }


def reference(x, positions, attn_norm, ffn_norm, final_norm,
              mla_w_dq, mla_q_norm, mla_w_uq, mla_w_dkv, mla_kv_norm, mla_w_ukv, mla_w_o,
              diff_kv_norm, diff_w_k, diff_w_v, diff_w_q,
              diff_lambda_q1, diff_lambda_k1, diff_lambda_q2, diff_lambda_k2,
              diff_subln, diff_w_o, ffn_w_gate_up, ffn_w_down):
    h = x
    shared = None
    for l in range(DEPTH):
        hn = _rmsnorm(h, attn_norm[l])
        if l < N_A_LAYERS:
            i = l
            h = h + _mla(hn, positions, mla_w_dq[i], mla_q_norm[i], mla_w_uq[i],
                         mla_w_dkv[i], mla_kv_norm[i], mla_w_ukv[i], mla_w_o[i])
        else:
            i = l - N_A_LAYERS
            if shared is None:
                shared = _shared_kv(h, diff_kv_norm, diff_w_k, diff_w_v)
                hn = _rmsnorm(h, attn_norm[l])
            k1, k2, v = shared
            lambda_init = 0.8 - 0.6 * float(np.exp(-0.3 * l))
            h = h + _diff_attn(hn, positions, k1, k2, v, diff_w_q[i],
                               diff_lambda_q1[i], diff_lambda_k1[i],
                               diff_lambda_q2[i], diff_lambda_k2[i],
                               diff_subln[i], diff_w_o[i], lambda_init)
        h = h + _swiglu(_rmsnorm(h, ffn_norm[l]), ffn_w_gate_up[l], ffn_w_down[l])
    return _rmsnorm(h, final_norm)
```

```python
import functools
import math

import jax
import jax.numpy as jnp
import numpy as np
from jax import lax
from jax.experimental import pallas as pl
from jax.experimental.pallas import tpu as pltpu

D_MODEL = 1024
SEQ = 16384
MLA_HEADS = 8
QK_NOPE = 128
QK_ROPE = 64
V_HEAD = 128
Q_LORA = 512
KV_LORA = 256
ROPE_THETA = 10000.0
DIFF_HEADS = 8
DIFF_HEAD = 64
DIFF_V = 2 * DIFF_HEAD
D_FF = 2816
EPS = 1e-6
NEG_INF = -1e30
LAMBDA_INIT = 0.8 - 0.6 * float(np.exp(-0.3 * 1))

LANE = 128
HEAD_PAD = 2 * LANE
LOG2E = math.log2(math.e)
VMEM_LIMIT = 56 * 1024 * 1024

ROW_BLOCK = 512
ATTN_BLOCK = 512
FF_CHUNK = D_FF // 2

POS_SHIFT_HI = 14
POS_SHIFT_MID = 7
N_BIAS_DIMS = 6

BF16 = jnp.bfloat16
F32 = jnp.float32


def _dot(a, b):
    return jnp.dot(a, b, preferred_element_type=F32)


def _dot_nt(a, b):
    return lax.dot_general(a, b, (((1,), (1,)), ((), ())), preferred_element_type=F32)


def _rms_scale(x):
    return lax.rsqrt(jnp.mean(x * x, axis=-1, keepdims=True) + EPS)


def _full(shape):
    return pl.BlockSpec(shape, lambda *_: (0,) * len(shape))


def _resident(shape):
    return pl.BlockSpec(shape, lambda *_: (0,) * len(shape), pipeline_mode=pl.Buffered(1))


def _rope_tile(t, cos_t, sin_up, sin_dn):
    return t * cos_t + pltpu.roll(t, 32, 1) * sin_up + pltpu.roll(t, 96, 1) * sin_dn


def _mla_proj_kernel(x_ref, pos_ref, inv_ref, g_attn_ref, w_dq_ref, g_q_ref, w_uq_ref,
                     w_dkv_ref, g_kv_ref, w_ukv_ref, q_ref, k_ref, v_ref):
    x = x_ref[...]
    hn = ((x * _rms_scale(x)) * g_attn_ref[...]).astype(BF16)

    cq = _dot(hn, w_dq_ref[...])
    cqn = ((cq * _rms_scale(cq)) * g_q_ref[...]).astype(BF16)
    qf = _dot(cqn, w_uq_ref[...])

    ckv = _dot(hn, w_dkv_ref[...])
    c = ckv[:, :KV_LORA]
    c_kv = ((c * _rms_scale(c)) * g_kv_ref[...]).astype(BF16)
    kv = _dot(c_kv, w_ukv_ref[...])

    ang = pos_ref[...] * inv_ref[...]
    lane = lax.broadcasted_iota(jnp.int32, ang.shape, 1)
    cos_a = jnp.cos(ang)
    sin_a = jnp.sin(ang)
    zero = jnp.zeros_like(ang)
    cos_t = jnp.where(lane < QK_ROPE, cos_a, zero)
    sin_up = jnp.where((lane >= QK_ROPE // 2) & (lane < QK_ROPE), sin_a, zero)
    sin_dn = jnp.where(lane < QK_ROPE // 2, -sin_a, zero)

    k_pe = _rope_tile(ckv[:, KV_LORA:], cos_t, sin_up, sin_dn).astype(BF16)
    q_scale = (QK_NOPE + QK_ROPE) ** -0.5 * LOG2E
    for h in range(MLA_HEADS):
        base = h * HEAD_PAD
        q_ref[h, :, :LANE] = (qf[:, base:base + LANE] * q_scale).astype(BF16)
        q_rope = _rope_tile(qf[:, base + LANE:base + HEAD_PAD], cos_t, sin_up, sin_dn)
        q_ref[h, :, LANE:] = (q_rope * q_scale).astype(BF16)
        k_ref[h, :, :LANE] = kv[:, h * LANE:(h + 1) * LANE].astype(BF16)
        k_ref[h, :, LANE:] = k_pe
        v_off = MLA_HEADS * LANE + h * V_HEAD
        v_ref[h] = kv[:, v_off:v_off + V_HEAD].astype(BF16)


def _mla_proj(x, pos_f, inv_lane, g_attn, w_dq, g_q, w_uq, w_dkv, g_kv, w_ukv):
    tm = ROW_BLOCK
    row = lambda i: (i, 0)
    head_rows = lambda i: (0, i, 0)
    return pl.pallas_call(
        _mla_proj_kernel,
        grid=(SEQ // tm,),
        in_specs=[
            pl.BlockSpec((tm, D_MODEL), row),
            pl.BlockSpec((tm, 1), row),
            _full((1, LANE)),
            _full((1, D_MODEL)),
            _resident(w_dq.shape),
            _full((1, Q_LORA)),
            _resident(w_uq.shape),
            _resident(w_dkv.shape),
            _full((1, KV_LORA)),
            _resident(w_ukv.shape),
        ],
        out_specs=[
            pl.BlockSpec((MLA_HEADS, tm, HEAD_PAD), head_rows),
            pl.BlockSpec((MLA_HEADS, tm, HEAD_PAD), head_rows),
            pl.BlockSpec((MLA_HEADS, tm, V_HEAD), head_rows),
        ],
        out_shape=[
            jax.ShapeDtypeStruct((MLA_HEADS, SEQ, HEAD_PAD), BF16),
            jax.ShapeDtypeStruct((MLA_HEADS, SEQ, HEAD_PAD), BF16),
            jax.ShapeDtypeStruct((MLA_HEADS, SEQ, V_HEAD), BF16),
        ],
        compiler_params=pltpu.CompilerParams(
            dimension_semantics=("arbitrary",), vmem_limit_bytes=VMEM_LIMIT),
        name="mla_proj",
    )(x, pos_f, inv_lane, g_attn, w_dq, g_q, w_uq, w_dkv, g_kv, w_ukv)


def _online_softmax_step(s, v_blk, m_ref, l_ref, acc_ref):
    m_prev = m_ref[...]
    m_next = jnp.maximum(m_prev, jnp.max(s, axis=1, keepdims=True))
    alpha = jnp.exp2(m_prev - m_next)
    p = jnp.exp2(s - m_next)
    l_ref[...] = alpha * l_ref[...] + jnp.sum(p, axis=1, keepdims=True)
    acc_ref[...] = alpha * acc_ref[...] + _dot(p.astype(BF16), v_blk)
    m_ref[...] = m_next


def _causal_mask(s):
    row = lax.broadcasted_iota(jnp.int32, s.shape, 0)
    col = lax.broadcasted_iota(jnp.int32, s.shape, 1)
    return jnp.where(row >= col, s, NEG_INF)


def _init_stats(m_ref, l_ref, acc_ref):
    m_ref[...] = jnp.full(m_ref.shape, NEG_INF, F32)
    l_ref[...] = jnp.zeros(l_ref.shape, F32)
    acc_ref[...] = jnp.zeros(acc_ref.shape, F32)


def _mla_attn_kernel(q_ref, k_ref, v_ref, o_ref, m_ref, l_ref, acc_ref):
    i = pl.program_id(1)
    t = ATTN_BLOCK
    _init_stats(m_ref, l_ref, acc_ref)
    q = q_ref[0]

    def step(j, masked):
        start = pl.multiple_of(j * t, t)
        s = _dot_nt(q, k_ref[0, pl.ds(start, t), :])
        if masked:
            s = _causal_mask(s)
        _online_softmax_step(s, v_ref[0, pl.ds(start, t), :], m_ref, l_ref, acc_ref)

    def body(j, carry):
        step(j, False)
        return carry

    lax.fori_loop(0, i, body, 0)
    step(i, True)
    o_ref[...] = (acc_ref[...] / l_ref[...]).astype(o_ref.dtype)


def _mla_attn(q, k, v):
    t = ATTN_BLOCK
    return pl.pallas_call(
        _mla_attn_kernel,
        grid=(MLA_HEADS, SEQ // t),
        in_specs=[
            pl.BlockSpec((1, t, HEAD_PAD), lambda h, i: (h, i, 0)),
            pl.BlockSpec((1, SEQ, HEAD_PAD), lambda h, i: (h, 0, 0)),
            pl.BlockSpec((1, SEQ, V_HEAD), lambda h, i: (h, 0, 0)),
        ],
        out_specs=pl.BlockSpec((t, V_HEAD), lambda h, i: (i, h)),
        out_shape=jax.ShapeDtypeStruct((SEQ, MLA_HEADS * V_HEAD), BF16),
        scratch_shapes=[
            pltpu.VMEM((t, 1), F32),
            pltpu.VMEM((t, 1), F32),
            pltpu.VMEM((t, V_HEAD), F32),
        ],
        compiler_params=pltpu.CompilerParams(
            dimension_semantics=("arbitrary", "arbitrary"), vmem_limit_bytes=VMEM_LIMIT),
        name="mla_attn",
    )(q, k, v)


def _ffn_kernel(h_ref, o_ref, w_o_ref, g_ffn_ref, w_gate_ref, w_up_ref, w_down_ref,
                g_final_ref, out_ref, *, final_norm):
    h1 = h_ref[...] + _dot(o_ref[...], w_o_ref[...])
    hn = ((h1 * _rms_scale(h1)) * g_ffn_ref[...]).astype(BF16)
    y = h1
    for c in range(D_FF // FF_CHUNK):
        g = _dot(hn, w_gate_ref[c])
        u = _dot(hn, w_up_ref[c])
        act = (g * (1.0 / (1.0 + jnp.exp(-g))) * u).astype(BF16)
        y = y + _dot(act, w_down_ref[c])
    if final_norm:
        y = (y * _rms_scale(y)) * g_final_ref[...]
    out_ref[...] = y


def _ffn(h, o, w_o, g_ffn, w_gate, w_up, w_down, g_final, *, final_norm):
    tm = ROW_BLOCK
    row = lambda i: (i, 0)
    return pl.pallas_call(
        functools.partial(_ffn_kernel, final_norm=final_norm),
        grid=(SEQ // tm,),
        in_specs=[
            pl.BlockSpec((tm, D_MODEL), row),
            pl.BlockSpec((tm, D_MODEL), row),
            _resident(w_o.shape),
            _full((1, D_MODEL)),
            _resident(w_gate.shape),
            _resident(w_up.shape),
            _resident(w_down.shape),
            _full((1, D_MODEL)),
        ],
        out_specs=pl.BlockSpec((tm, D_MODEL), row),
        out_shape=jax.ShapeDtypeStruct((SEQ, D_MODEL), F32),
        compiler_params=pltpu.CompilerParams(
            dimension_semantics=("arbitrary",), vmem_limit_bytes=VMEM_LIMIT),
        name="ffn_final" if final_norm else "ffn",
    )(h, o, w_o, g_ffn, w_gate, w_up, w_down, g_final)


def _pos_pieces(pos):
    a = pos >> POS_SHIFT_HI
    b = (pos >> POS_SHIFT_MID) & ((1 << (POS_SHIFT_HI - POS_SHIFT_MID)) - 1)
    c = pos & ((1 << POS_SHIFT_MID) - 1)
    return ((a << POS_SHIFT_HI).astype(F32), (b << POS_SHIFT_MID).astype(F32), c.astype(F32))


def _bias_tile(pos, lane, q_side):
    pa, pb, pc = _pos_pieces(pos)
    d = lane - DIFF_HEAD
    one = jnp.ones(lane.shape, F32)
    zero = jnp.zeros(lane.shape, F32)
    if q_side:
        vals = (one, one, one, -pa, -pb, -pc)
    else:
        vals = (pa, pb, pc, one, one, one)
    out = zero
    for n, v in enumerate(vals):
        out = jnp.where(d == n, v, out)
    return out


def _diff_proj_kernel(h_ref, pos_ref, g_kv_ref, g_attn_ref, w_q_ref, w_k_ref, w_v_ref,
                      q_ref, k_ref, v_ref):
    h = h_ref[...]
    hr = h * _rms_scale(h)
    hk = (hr * g_kv_ref[...]).astype(BF16)
    hq = (hr * g_attn_ref[...]).astype(BF16)
    qf = _dot(hq, w_q_ref[...])
    kf = _dot(hk, w_k_ref[...])
    vf = _dot(hk, w_v_ref[...])

    pos = pos_ref[...]
    lane = lax.broadcasted_iota(jnp.int32, (pos.shape[0], LANE), 1)
    q_bias = _bias_tile(pos, lane, True)
    k_bias = _bias_tile(pos, lane, False)
    q_scale = DIFF_HEAD ** -0.5
    for hd in range(DIFF_HEADS):
        slope = 2.0 ** (-8.0 * (hd + 1) / DIFF_HEADS)
        for half in range(2):
            lo = hd * HEAD_PAD + half * LANE
            q_ref[hd, :, half * LANE:(half + 1) * LANE] = (
                qf[:, lo:lo + LANE] * q_scale + q_bias).astype(BF16)
            k_ref[hd, :, half * LANE:(half + 1) * LANE] = (
                kf[:, lo:lo + LANE] + k_bias * slope).astype(BF16)
        v_ref[hd] = vf[:, hd * DIFF_V:(hd + 1) * DIFF_V].astype(BF16)


def _diff_proj(h, pos_i, g_kv, g_attn, w_q, w_k, w_v):
    tm = ROW_BLOCK
    row = lambda i: (i, 0)
    head_rows = lambda i: (0, i, 0)
    return pl.pallas_call(
        _diff_proj_kernel,
        grid=(SEQ // tm,),
        in_specs=[
            pl.BlockSpec((tm, D_MODEL), row),
            pl.BlockSpec((tm, 1), row),
            _full((1, D_MODEL)),
            _full((1, D_MODEL)),
            _resident(w_q.shape),
            _resident(w_k.shape),
            _resident(w_v.shape),
        ],
        out_specs=[
            pl.BlockSpec((DIFF_HEADS, tm, HEAD_PAD), head_rows),
            pl.BlockSpec((DIFF_HEADS, tm, HEAD_PAD), head_rows),
            pl.BlockSpec((DIFF_HEADS, tm, DIFF_V), head_rows),
        ],
        out_shape=[
            jax.ShapeDtypeStruct((DIFF_HEADS, SEQ, HEAD_PAD), BF16),
            jax.ShapeDtypeStruct((DIFF_HEADS, SEQ, HEAD_PAD), BF16),
            jax.ShapeDtypeStruct((DIFF_HEADS, SEQ, DIFF_V), BF16),
        ],
        compiler_params=pltpu.CompilerParams(
            dimension_semantics=("arbitrary",), vmem_limit_bytes=VMEM_LIMIT),
        name="diff_proj",
    )(h, pos_i, g_kv, g_attn, w_q, w_k, w_v)


def _diff_attn_kernel(lam_ref, q_ref, k_ref, v_ref, subln_ref, o_ref,
                      m1_ref, l1_ref, acc1_ref, m2_ref, l2_ref, acc2_ref):
    i = pl.program_id(1)
    t = ATTN_BLOCK
    _init_stats(m1_ref, l1_ref, acc1_ref)
    _init_stats(m2_ref, l2_ref, acc2_ref)
    q1 = q_ref[0, :, :LANE]
    q2 = q_ref[0, :, LANE:]

    def step(j, masked):
        start = pl.multiple_of(j * t, t)
        v_blk = v_ref[0, pl.ds(start, t), :]
        for q, half, stats in ((q1, 0, (m1_ref, l1_ref, acc1_ref)),
                               (q2, 1, (m2_ref, l2_ref, acc2_ref))):
            s = _dot_nt(q, k_ref[0, pl.ds(start, t), half * LANE:(half + 1) * LANE]) * LOG2E
            if masked:
                s = _causal_mask(s)
            _online_softmax_step(s, v_blk, *stats)

    def body(j, carry):
        step(j, False)
        return carry

    lax.fori_loop(0, i, body, 0)
    step(i, True)

    lp = lam_ref[...]
    lam = (jnp.exp(jnp.sum(lp[0:1] * lp[1:2], axis=1, keepdims=True))
           - jnp.exp(jnp.sum(lp[2:3] * lp[3:4], axis=1, keepdims=True)) + LAMBDA_INIT)
    o = acc1_ref[...] / l1_ref[...] - lam * (acc2_ref[...] / l2_ref[...])
    o = (o * _rms_scale(o)) * subln_ref[...] * (1.0 - LAMBDA_INIT)
    o_ref[...] = o.astype(o_ref.dtype)


def _diff_attn(lam, q, k, v, subln):
    t = ATTN_BLOCK
    stat = pltpu.VMEM((t, 1), F32)
    acc = pltpu.VMEM((t, DIFF_V), F32)
    return pl.pallas_call(
        _diff_attn_kernel,
        grid=(DIFF_HEADS, SEQ // t),
        in_specs=[
            pl.BlockSpec((4, DIFF_HEAD), lambda h, i: (0, 0)),
            pl.BlockSpec((1, t, HEAD_PAD), lambda h, i: (h, i, 0)),
            pl.BlockSpec((1, SEQ, HEAD_PAD), lambda h, i: (h, 0, 0)),
            pl.BlockSpec((1, SEQ, DIFF_V), lambda h, i: (h, 0, 0)),
            pl.BlockSpec((1, DIFF_V), lambda h, i: (0, 0)),
        ],
        out_specs=pl.BlockSpec((t, DIFF_V), lambda h, i: (i, h)),
        out_shape=jax.ShapeDtypeStruct((SEQ, DIFF_HEADS * DIFF_V), BF16),
        scratch_shapes=[stat, stat, acc, stat, stat, acc],
        compiler_params=pltpu.CompilerParams(
            dimension_semantics=("arbitrary", "arbitrary"), vmem_limit_bytes=VMEM_LIMIT),
        name="diff_attn",
    )(lam, q, k, v, subln)


def _pad_last(a, width):
    return jnp.pad(a, [(0, 0)] * (a.ndim - 1) + [(0, width - a.shape[-1])])


def _split_ffn(w_gate_up, w_down):
    n = D_FF // FF_CHUNK
    w_gate = w_gate_up[:, :D_FF].reshape(D_MODEL, n, FF_CHUNK).transpose(1, 0, 2)
    w_up = w_gate_up[:, D_FF:].reshape(D_MODEL, n, FF_CHUNK).transpose(1, 0, 2)
    return w_gate.astype(BF16), w_up.astype(BF16), w_down.reshape(n, FF_CHUNK, D_MODEL).astype(BF16)


def kernel(x, positions, attn_norm, ffn_norm, final_norm, mla_w_dq, mla_q_norm, mla_w_uq,
           mla_w_dkv, mla_kv_norm, mla_w_ukv, mla_w_o, diff_kv_norm, diff_w_k, diff_w_v,
           diff_w_q, diff_lambda_q1, diff_lambda_k1, diff_lambda_q2, diff_lambda_k2,
           diff_subln, diff_w_o, ffn_w_gate_up, ffn_w_down):
    x2 = x.reshape(SEQ, D_MODEL)
    pos_i = positions.reshape(SEQ, 1)
    pos_f = pos_i.astype(F32)
    row_vec = lambda a: a.reshape(1, -1)

    half = QK_ROPE // 2
    inv = ROPE_THETA ** (-jnp.arange(half, dtype=F32) * 2.0 / QK_ROPE)
    inv_lane = jnp.tile(inv, LANE // half).reshape(1, LANE)

    w_uq = mla_w_uq[0].reshape(Q_LORA, MLA_HEADS, QK_NOPE + QK_ROPE)
    w_uq = _pad_last(w_uq, HEAD_PAD).reshape(Q_LORA, MLA_HEADS * HEAD_PAD).astype(BF16)
    w_dkv = _pad_last(mla_w_dkv[0], KV_LORA + LANE).astype(BF16)
    w_ukv = mla_w_ukv[0].reshape(KV_LORA, MLA_HEADS, QK_NOPE + V_HEAD)
    w_ukv = jnp.concatenate(
        [w_ukv[:, :, :QK_NOPE].reshape(KV_LORA, -1), w_ukv[:, :, QK_NOPE:].reshape(KV_LORA, -1)],
        axis=1).astype(BF16)
    q, k, v = _mla_proj(x2, pos_f, inv_lane, row_vec(attn_norm[0]), mla_w_dq[0].astype(BF16),
                        row_vec(mla_q_norm[0]), w_uq, w_dkv, row_vec(mla_kv_norm[0]), w_ukv)
    o = _mla_attn(q, k, v)
    w_gate, w_up, w_down = _split_ffn(ffn_w_gate_up[0], ffn_w_down[0])
    h = _ffn(x2, o, mla_w_o[0].astype(BF16), row_vec(ffn_norm[0]), w_gate, w_up, w_down,
             row_vec(final_norm), final_norm=False)

    def pad_halves(w):
        w = w.reshape(D_MODEL, DIFF_HEADS, 2, DIFF_HEAD)
        return _pad_last(w, LANE).reshape(D_MODEL, DIFF_HEADS * HEAD_PAD).astype(BF16)

    qd, kd, vd = _diff_proj(h, pos_i, row_vec(diff_kv_norm), row_vec(attn_norm[1]),
                            pad_halves(diff_w_q[0]), pad_halves(diff_w_k), diff_w_v.astype(BF16))
    lam = jnp.stack([diff_lambda_q1[0], diff_lambda_k1[0], diff_lambda_q2[0], diff_lambda_k2[0]])
    od = _diff_attn(lam, qd, kd, vd, row_vec(diff_subln[0]))
    w_gate, w_up, w_down = _split_ffn(ffn_w_gate_up[1], ffn_w_down[1])
    out = _ffn(h, od, diff_w_o[0].astype(BF16), row_vec(ffn_norm[1]), w_gate, w_up, w_down,
               row_vec(final_norm), final_norm=True)
    return out.reshape(x.shape)
```

```python
import functools
import math

import jax
import jax.numpy as jnp
import numpy as np
from jax import lax
from jax.experimental import pallas as pl
from jax.experimental.pallas import tpu as pltpu

D_MODEL = 1024
SEQ = 16384
MLA_HEADS = 8
QK_NOPE = 128
QK_ROPE = 64
V_HEAD = 128
Q_LORA = 512
KV_LORA = 256
ROPE_THETA = 10000.0
DIFF_HEADS = 8
DIFF_HEAD = 64
DIFF_V = 2 * DIFF_HEAD
D_FF = 2816
EPS = 1e-6
NEG_INF = -1e30
LAMBDA_INIT = 0.8 - 0.6 * float(np.exp(-0.3 * 1))

LANE = 128
HEAD_PAD = 2 * LANE
LOG2E = math.log2(math.e)
VMEM_LIMIT = 56 * 1024 * 1024

ROW_BLOCK = 512
ATTN_Q_BLOCK = 1024
ATTN_K_BLOCK = 256
FF_CHUNK = D_FF // 2
BF16_SUBLANES = 16
V_ROWS = V_HEAD + BF16_SUBLANES

POS_SHIFT_HI = 14
POS_SHIFT_MID = 7
N_BIAS_DIMS = 6

BF16 = jnp.bfloat16
F32 = jnp.float32


def _dot(a, b):
    return jnp.dot(a, b, preferred_element_type=F32)


def _dot_nt(a, b):
    return lax.dot_general(a, b, (((1,), (1,)), ((), ())), preferred_element_type=F32)


def _rms_scale(x):
    return lax.rsqrt(jnp.mean(x * x, axis=-1, keepdims=True) + EPS)


def _full(shape):
    return pl.BlockSpec(shape, lambda *_: (0,) * len(shape))


def _resident(shape):
    return pl.BlockSpec(shape, lambda *_: (0,) * len(shape), pipeline_mode=pl.Buffered(1))


def _rope_tile(t, cos_t, sin_up, sin_dn):
    return t * cos_t + pltpu.roll(t, 32, 1) * sin_up + pltpu.roll(t, 96, 1) * sin_dn


def _mla_proj_kernel(x_ref, pos_ref, inv_ref, g_attn_ref, w_dq_ref, g_q_ref, w_uq_ref,
                     w_dkv_ref, g_kv_ref, w_ukv_ref, q_ref, k_ref, v_ref):
    x = x_ref[...]
    hn = ((x * _rms_scale(x)) * g_attn_ref[...]).astype(BF16)

    cq = _dot(hn, w_dq_ref[...])
    cqn = ((cq * _rms_scale(cq)) * g_q_ref[...]).astype(BF16)
    qf = _dot(cqn, w_uq_ref[...])

    ckv = _dot(hn, w_dkv_ref[...])
    c = ckv[:, :KV_LORA]
    c_kv = ((c * _rms_scale(c)) * g_kv_ref[...]).astype(BF16)
    kv = _dot(c_kv, w_ukv_ref[...])

    ang = pos_ref[...] * inv_ref[...]
    lane = lax.broadcasted_iota(jnp.int32, ang.shape, 1)
    cos_a = jnp.cos(ang)
    sin_a = jnp.sin(ang)
    zero = jnp.zeros_like(ang)
    cos_t = jnp.where(lane < QK_ROPE, cos_a, zero)
    sin_up = jnp.where((lane >= QK_ROPE // 2) & (lane < QK_ROPE), sin_a, zero)
    sin_dn = jnp.where(lane < QK_ROPE // 2, -sin_a, zero)

    k_pe = _rope_tile(ckv[:, KV_LORA:], cos_t, sin_up, sin_dn).astype(BF16)
    q_scale = (QK_NOPE + QK_ROPE) ** -0.5 * LOG2E
    for h in range(MLA_HEADS):
        base = h * HEAD_PAD
        q_ref[h, :, :LANE] = (qf[:, base:base + LANE] * q_scale).astype(BF16)
        q_rope = _rope_tile(qf[:, base + LANE:base + HEAD_PAD], cos_t, sin_up, sin_dn)
        q_ref[h, :, LANE:] = (q_rope * q_scale).astype(BF16)
        k_ref[h, :, :LANE] = kv[:, h * LANE:(h + 1) * LANE].astype(BF16)
        k_ref[h, :, LANE:] = k_pe
        v_off = MLA_HEADS * LANE + h * V_HEAD
        v_ref[h] = kv[:, v_off:v_off + V_HEAD].astype(BF16)


def _mla_proj(x, pos_f, inv_lane, g_attn, w_dq, g_q, w_uq, w_dkv, g_kv, w_ukv):
    tm = ROW_BLOCK
    row = lambda i: (i, 0)
    head_rows = lambda i: (0, i, 0)
    return pl.pallas_call(
        _mla_proj_kernel,
        grid=(SEQ // tm,),
        in_specs=[
            pl.BlockSpec((tm, D_MODEL), row),
            pl.BlockSpec((tm, 1), row),
            _full((1, LANE)),
            _full((1, D_MODEL)),
            _resident(w_dq.shape),
            _full((1, Q_LORA)),
            _resident(w_uq.shape),
            _resident(w_dkv.shape),
            _full((1, KV_LORA)),
            _resident(w_ukv.shape),
        ],
        out_specs=[
            pl.BlockSpec((MLA_HEADS, tm, HEAD_PAD), head_rows),
            pl.BlockSpec((MLA_HEADS, tm, HEAD_PAD), head_rows),
            pl.BlockSpec((MLA_HEADS, tm, V_HEAD), head_rows),
        ],
        out_shape=[
            jax.ShapeDtypeStruct((MLA_HEADS, SEQ, HEAD_PAD), BF16),
            jax.ShapeDtypeStruct((MLA_HEADS, SEQ, HEAD_PAD), BF16),
            jax.ShapeDtypeStruct((MLA_HEADS, SEQ, V_HEAD), BF16),
        ],
        compiler_params=pltpu.CompilerParams(
            dimension_semantics=("arbitrary",), vmem_limit_bytes=VMEM_LIMIT),
        name="mla_proj",
    )(x, pos_f, inv_lane, g_attn, w_dq, g_q, w_uq, w_dkv, g_kv, w_ukv)


def _online_softmax_step(s, vt_blk, m, acc_ref):
    m_next = jnp.maximum(m, jnp.max(s, axis=0, keepdims=True))
    alpha = jnp.exp2(m - m_next)
    p = jnp.exp2(s - m_next)
    acc_ref[...] = alpha * acc_ref[...] + _dot(vt_blk, p.astype(BF16))
    return m_next


def _causal_mask(s, k_offset):
    row = lax.broadcasted_iota(jnp.int32, s.shape, 0)
    col = lax.broadcasted_iota(jnp.int32, s.shape, 1)
    return jnp.where(row + k_offset <= col, s, NEG_INF)


def _normalized(acc_ref):
    return acc_ref[:DIFF_V, :] / acc_ref[DIFF_V:DIFF_V + 1, :]


def _causal_key_loop(i, scores, consume, s_refs, stats):
    group = ATTN_Q_BLOCK // ATTN_K_BLOCK

    def put(vals):
        for ref, val in zip(s_refs, vals):
            ref[...] = val

    def get():
        return tuple(ref[...] for ref in s_refs)

    put(scores(0))

    def full_group(g, stats):
        s_cur = get()
        for d in range(group):
            j = g * group + d
            s_next = scores(j + 1)
            stats = consume(s_cur, j, stats, None)
            s_cur = s_next
        put(s_cur)
        return stats

    stats = lax.fori_loop(0, i, full_group, stats)
    s_cur = get()
    for d in range(group):
        j = i * group + d
        s_next = scores(j + 1) if d + 1 < group else None
        stats = consume(s_cur, j, stats, d * ATTN_K_BLOCK)
        s_cur = s_next
    return stats


def _mla_attn_kernel(qt_ref, k_ref, vt_ref, o_ref, acc_ref, s_ref):
    i = pl.program_id(1)
    tq, tk = ATTN_Q_BLOCK, ATTN_K_BLOCK
    acc_ref[...] = jnp.zeros(acc_ref.shape, F32)

    def scores(j):
        start = pl.multiple_of(j * tk, tk)
        return (_dot(k_ref[0, pl.ds(start, tk), :], qt_ref[0]),)

    def consume(s, j, m, k_offset):
        s = s[0] if k_offset is None else _causal_mask(s[0], k_offset)
        return _online_softmax_step(s, vt_ref[0, j], m, acc_ref)

    _causal_key_loop(i, scores, consume, (s_ref,), jnp.full((1, tq), NEG_INF, F32))
    o_ref[...] = _normalized(acc_ref).T.astype(o_ref.dtype)


def _mla_attn(qt, k, vt):
    tq, tk = ATTN_Q_BLOCK, ATTN_K_BLOCK
    return pl.pallas_call(
        _mla_attn_kernel,
        grid=(MLA_HEADS, SEQ // tq),
        in_specs=[
            pl.BlockSpec((1, HEAD_PAD, tq), lambda h, i: (h, 0, i)),
            pl.BlockSpec((1, SEQ, HEAD_PAD), lambda h, i: (h, 0, 0)),
            pl.BlockSpec((1, SEQ // tk, V_ROWS, tk), lambda h, i: (h, 0, 0, 0)),
        ],
        out_specs=pl.BlockSpec((tq, V_HEAD), lambda h, i: (i, h)),
        out_shape=jax.ShapeDtypeStruct((SEQ, MLA_HEADS * V_HEAD), BF16),
        scratch_shapes=[pltpu.VMEM((V_ROWS, tq), F32), pltpu.VMEM((tk, tq), F32)],
        compiler_params=pltpu.CompilerParams(
            dimension_semantics=("arbitrary", "arbitrary"), vmem_limit_bytes=VMEM_LIMIT),
        name="mla_attn",
    )(qt, k, vt)


def _ffn_kernel(h_ref, o_ref, w_o_ref, g_ffn_ref, w_gate_ref, w_up_ref, w_down_ref,
                g_final_ref, out_ref, *, final_norm):
    h1 = h_ref[...] + _dot(o_ref[...], w_o_ref[...])
    hn = ((h1 * _rms_scale(h1)) * g_ffn_ref[...]).astype(BF16)
    y = h1
    for c in range(D_FF // FF_CHUNK):
        g = _dot(hn, w_gate_ref[c])
        u = _dot(hn, w_up_ref[c])
        act = (g * (1.0 / (1.0 + jnp.exp(-g))) * u).astype(BF16)
        y = y + _dot(act, w_down_ref[c])
    if final_norm:
        y = (y * _rms_scale(y)) * g_final_ref[...]
    out_ref[...] = y


def _ffn(h, o, w_o, g_ffn, w_gate, w_up, w_down, g_final, *, final_norm):
    tm = ROW_BLOCK
    row = lambda i: (i, 0)
    return pl.pallas_call(
        functools.partial(_ffn_kernel, final_norm=final_norm),
        grid=(SEQ // tm,),
        in_specs=[
            pl.BlockSpec((tm, D_MODEL), row),
            pl.BlockSpec((tm, D_MODEL), row),
            _resident(w_o.shape),
            _full((1, D_MODEL)),
            _resident(w_gate.shape),
            _resident(w_up.shape),
            _resident(w_down.shape),
            _full((1, D_MODEL)),
        ],
        out_specs=pl.BlockSpec((tm, D_MODEL), row),
        out_shape=jax.ShapeDtypeStruct((SEQ, D_MODEL), F32),
        compiler_params=pltpu.CompilerParams(
            dimension_semantics=("arbitrary",), vmem_limit_bytes=VMEM_LIMIT),
        name="ffn_final" if final_norm else "ffn",
    )(h, o, w_o, g_ffn, w_gate, w_up, w_down, g_final)


def _pos_pieces(pos):
    a = pos >> POS_SHIFT_HI
    b = (pos >> POS_SHIFT_MID) & ((1 << (POS_SHIFT_HI - POS_SHIFT_MID)) - 1)
    c = pos & ((1 << POS_SHIFT_MID) - 1)
    return ((a << POS_SHIFT_HI).astype(F32), (b << POS_SHIFT_MID).astype(F32), c.astype(F32))


def _bias_tile(pos, lane, q_side):
    pa, pb, pc = _pos_pieces(pos)
    d = lane - DIFF_HEAD
    one = jnp.ones(lane.shape, F32)
    zero = jnp.zeros(lane.shape, F32)
    if q_side:
        vals = (one, one, one, -pa, -pb, -pc)
    else:
        vals = (pa, pb, pc, one, one, one)
    out = zero
    for n, v in enumerate(vals):
        out = jnp.where(d == n, v, out)
    return out


def _diff_proj_kernel(h_ref, pos_ref, g_kv_ref, g_attn_ref, w_q_ref, w_k_ref, w_v_ref,
                      q_ref, k_ref, v_ref):
    h = h_ref[...]
    hr = h * _rms_scale(h)
    hk = (hr * g_kv_ref[...]).astype(BF16)
    hq = (hr * g_attn_ref[...]).astype(BF16)
    qf = _dot(hq, w_q_ref[...])
    kf = _dot(hk, w_k_ref[...])
    vf = _dot(hk, w_v_ref[...])

    pos = pos_ref[...]
    lane = lax.broadcasted_iota(jnp.int32, (pos.shape[0], LANE), 1)
    q_bias = _bias_tile(pos, lane, True)
    k_bias = _bias_tile(pos, lane, False)
    q_scale = DIFF_HEAD ** -0.5
    for hd in range(DIFF_HEADS):
        slope = 2.0 ** (-8.0 * (hd + 1) / DIFF_HEADS)
        for half in range(2):
            lo = hd * HEAD_PAD + half * LANE
            q_ref[hd, :, half * LANE:(half + 1) * LANE] = (
                qf[:, lo:lo + LANE] * q_scale + q_bias).astype(BF16)
            k_ref[hd, :, half * LANE:(half + 1) * LANE] = (
                kf[:, lo:lo + LANE] + k_bias * slope).astype(BF16)
        v_ref[hd] = vf[:, hd * DIFF_V:(hd + 1) * DIFF_V].astype(BF16)


def _diff_proj(h, pos_i, g_kv, g_attn, w_q, w_k, w_v):
    tm = ROW_BLOCK
    row = lambda i: (i, 0)
    head_rows = lambda i: (0, i, 0)
    return pl.pallas_call(
        _diff_proj_kernel,
        grid=(SEQ // tm,),
        in_specs=[
            pl.BlockSpec((tm, D_MODEL), row),
            pl.BlockSpec((tm, 1), row),
            _full((1, D_MODEL)),
            _full((1, D_MODEL)),
            _resident(w_q.shape),
            _resident(w_k.shape),
            _resident(w_v.shape),
        ],
        out_specs=[
            pl.BlockSpec((DIFF_HEADS, tm, HEAD_PAD), head_rows),
            pl.BlockSpec((DIFF_HEADS, tm, HEAD_PAD), head_rows),
            pl.BlockSpec((DIFF_HEADS, tm, DIFF_V), head_rows),
        ],
        out_shape=[
            jax.ShapeDtypeStruct((DIFF_HEADS, SEQ, HEAD_PAD), BF16),
            jax.ShapeDtypeStruct((DIFF_HEADS, SEQ, HEAD_PAD), BF16),
            jax.ShapeDtypeStruct((DIFF_HEADS, SEQ, DIFF_V), BF16),
        ],
        compiler_params=pltpu.CompilerParams(
            dimension_semantics=("arbitrary",), vmem_limit_bytes=VMEM_LIMIT),
        name="diff_proj",
    )(h, pos_i, g_kv, g_attn, w_q, w_k, w_v)


def _diff_attn_kernel(lam_ref, qt_ref, k_ref, vt_ref, subln_ref, o_ref,
                      acc1_ref, acc2_ref, s1_ref, s2_ref):
    i = pl.program_id(1)
    tq, tk = ATTN_Q_BLOCK, ATTN_K_BLOCK
    acc1_ref[...] = jnp.zeros(acc1_ref.shape, F32)
    acc2_ref[...] = jnp.zeros(acc2_ref.shape, F32)

    def scores(j):
        start = pl.multiple_of(j * tk, tk)
        return tuple(
            _dot(k_ref[0, pl.ds(start, tk), half * LANE:(half + 1) * LANE],
                 qt_ref[0, half * LANE:(half + 1) * LANE, :]) for half in range(2))

    def consume(s, j, m, k_offset):
        vt_blk = vt_ref[0, j]
        out = []
        for s_h, m_h, acc_ref in zip(s, m, (acc1_ref, acc2_ref)):
            s_h = s_h * LOG2E
            if k_offset is not None:
                s_h = _causal_mask(s_h, k_offset)
            out.append(_online_softmax_step(s_h, vt_blk, m_h, acc_ref))
        return tuple(out)

    m0 = jnp.full((1, tq), NEG_INF, F32)
    _causal_key_loop(i, scores, consume, (s1_ref, s2_ref), (m0, m0))

    lp = lam_ref[...]
    lam = (jnp.exp(jnp.sum(lp[0:1] * lp[1:2], axis=1, keepdims=True))
           - jnp.exp(jnp.sum(lp[2:3] * lp[3:4], axis=1, keepdims=True)) + LAMBDA_INIT)
    o = _normalized(acc1_ref) - lam * _normalized(acc2_ref)
    r = lax.rsqrt(jnp.mean(o * o, axis=0, keepdims=True) + EPS)
    o = (o * r).T * subln_ref[...] * (1.0 - LAMBDA_INIT)
    o_ref[...] = o.astype(o_ref.dtype)


def _diff_attn(lam, qt, k, vt, subln):
    tq, tk = ATTN_Q_BLOCK, ATTN_K_BLOCK
    acc = pltpu.VMEM((V_ROWS, tq), F32)
    s_tile = pltpu.VMEM((tk, tq), F32)
    return pl.pallas_call(
        _diff_attn_kernel,
        grid=(DIFF_HEADS, SEQ // tq),
        in_specs=[
            pl.BlockSpec((4, DIFF_HEAD), lambda h, i: (0, 0)),
            pl.BlockSpec((1, HEAD_PAD, tq), lambda h, i: (h, 0, i)),
            pl.BlockSpec((1, SEQ, HEAD_PAD), lambda h, i: (h, 0, 0)),
            pl.BlockSpec((1, SEQ // tk, V_ROWS, tk), lambda h, i: (h, 0, 0, 0)),
            pl.BlockSpec((1, DIFF_V), lambda h, i: (0, 0)),
        ],
        out_specs=pl.BlockSpec((tq, DIFF_V), lambda h, i: (i, h)),
        out_shape=jax.ShapeDtypeStruct((SEQ, DIFF_HEADS * DIFF_V), BF16),
        scratch_shapes=[acc, acc, s_tile, s_tile],
        compiler_params=pltpu.CompilerParams(
            dimension_semantics=("arbitrary", "arbitrary"), vmem_limit_bytes=VMEM_LIMIT),
        name="diff_attn",
    )(lam, qt, k, vt, subln)


def _pad_last(a, width):
    return jnp.pad(a, [(0, 0)] * (a.ndim - 1) + [(0, width - a.shape[-1])])


def _key_tiled_t(v):
    h, s, dv = v.shape
    tk = ATTN_K_BLOCK
    vt = jnp.swapaxes(v.reshape(h, s // tk, tk, dv), 2, 3)
    ones = jnp.ones((h, s // tk, 1, tk), v.dtype)
    pad = jnp.zeros((h, s // tk, V_ROWS - dv - 1, tk), v.dtype)
    return jnp.concatenate([vt, ones, pad], axis=2)


def _split_ffn(w_gate_up, w_down):
    n = D_FF // FF_CHUNK
    w_gate = w_gate_up[:, :D_FF].reshape(D_MODEL, n, FF_CHUNK).transpose(1, 0, 2)
    w_up = w_gate_up[:, D_FF:].reshape(D_MODEL, n, FF_CHUNK).transpose(1, 0, 2)
    return w_gate.astype(BF16), w_up.astype(BF16), w_down.reshape(n, FF_CHUNK, D_MODEL).astype(BF16)


def kernel(x, positions, attn_norm, ffn_norm, final_norm, mla_w_dq, mla_q_norm, mla_w_uq,
           mla_w_dkv, mla_kv_norm, mla_w_ukv, mla_w_o, diff_kv_norm, diff_w_k, diff_w_v,
           diff_w_q, diff_lambda_q1, diff_lambda_k1, diff_lambda_q2, diff_lambda_k2,
           diff_subln, diff_w_o, ffn_w_gate_up, ffn_w_down):
    x2 = x.reshape(SEQ, D_MODEL)
    pos_i = positions.reshape(SEQ, 1)
    pos_f = pos_i.astype(F32)
    row_vec = lambda a: a.reshape(1, -1)

    half = QK_ROPE // 2
    inv = ROPE_THETA ** (-jnp.arange(half, dtype=F32) * 2.0 / QK_ROPE)
    inv_lane = jnp.tile(inv, LANE // half).reshape(1, LANE)

    w_uq = mla_w_uq[0].reshape(Q_LORA, MLA_HEADS, QK_NOPE + QK_ROPE)
    w_uq = _pad_last(w_uq, HEAD_PAD).reshape(Q_LORA, MLA_HEADS * HEAD_PAD).astype(BF16)
    w_dkv = _pad_last(mla_w_dkv[0], KV_LORA + LANE).astype(BF16)
    w_ukv = mla_w_ukv[0].reshape(KV_LORA, MLA_HEADS, QK_NOPE + V_HEAD)
    w_ukv = jnp.concatenate(
        [w_ukv[:, :, :QK_NOPE].reshape(KV_LORA, -1), w_ukv[:, :, QK_NOPE:].reshape(KV_LORA, -1)],
        axis=1).astype(BF16)
    q, k, v = _mla_proj(x2, pos_f, inv_lane, row_vec(attn_norm[0]), mla_w_dq[0].astype(BF16),
                        row_vec(mla_q_norm[0]), w_uq, w_dkv, row_vec(mla_kv_norm[0]), w_ukv)
    o = _mla_attn(jnp.swapaxes(q, 1, 2), k, _key_tiled_t(v))
    w_gate, w_up, w_down = _split_ffn(ffn_w_gate_up[0], ffn_w_down[0])
    h = _ffn(x2, o, mla_w_o[0].astype(BF16), row_vec(ffn_norm[0]), w_gate, w_up, w_down,
             row_vec(final_norm), final_norm=False)

    def pad_halves(w):
        w = w.reshape(D_MODEL, DIFF_HEADS, 2, DIFF_HEAD)
        return _pad_last(w, LANE).reshape(D_MODEL, DIFF_HEADS * HEAD_PAD).astype(BF16)

    qd, kd, vd = _diff_proj(h, pos_i, row_vec(diff_kv_norm), row_vec(attn_norm[1]),
                            pad_halves(diff_w_q[0]), pad_halves(diff_w_k), diff_w_v.astype(BF16))
    lam = jnp.stack([diff_lambda_q1[0], diff_lambda_k1[0], diff_lambda_q2[0], diff_lambda_k2[0]])
    od = _diff_attn(lam, jnp.swapaxes(qd, 1, 2), kd, _key_tiled_t(vd), row_vec(diff_subln[0]))
    w_gate, w_up, w_down = _split_ffn(ffn_w_gate_up[1], ffn_w_down[1])
    out = _ffn(h, od, diff_w_o[0].astype(BF16), row_vec(ffn_norm[1]), w_gate, w_up, w_down,
               row_vec(final_norm), final_norm=True)
    return out.reshape(x.shape)
```

```python
import functools
import math

import jax
import jax.numpy as jnp
import numpy as np
from jax import lax
from jax.experimental import pallas as pl
from jax.experimental.pallas import tpu as pltpu

D_MODEL = 1024
SEQ = 16384
MLA_HEADS = 8
QK_NOPE = 128
QK_ROPE = 64
V_HEAD = 128
Q_LORA = 512
KV_LORA = 256
ROPE_THETA = 10000.0
DIFF_HEADS = 8
DIFF_HEAD = 64
DIFF_V = 2 * DIFF_HEAD
D_FF = 2816
EPS = 1e-6
NEG_INF = -1e30
LAMBDA_INIT = 0.8 - 0.6 * float(np.exp(-0.3 * 1))

LANE = 128
HEAD_PAD = 2 * LANE
LOG2E = math.log2(math.e)
VMEM_LIMIT = 56 * 1024 * 1024

ROW_BLOCK = 512
ATTN_Q_BLOCK = 1024
ATTN_K_BLOCK = 256
FF_CHUNK = D_FF // 2
BF16_SUBLANES = 16
V_ROWS = V_HEAD + BF16_SUBLANES

POS_SHIFT_HI = 14
POS_SHIFT_MID = 7
N_PIECES = 3
N_BIAS = N_PIECES * N_PIECES

BF16 = jnp.bfloat16
F32 = jnp.float32


def _dot(a, b):
    return jnp.dot(a, b, preferred_element_type=F32)


def _dot_nt(a, b):
    return lax.dot_general(a, b, (((1,), (1,)), ((), ())), preferred_element_type=F32)


def _rms_scale(x):
    return lax.rsqrt(jnp.mean(x * x, axis=-1, keepdims=True) + EPS)


def _full(shape):
    return pl.BlockSpec(shape, lambda *_: (0,) * len(shape))


def _resident(shape):
    return pl.BlockSpec(shape, lambda *_: (0,) * len(shape), pipeline_mode=pl.Buffered(1))


def _ones_row_group(width):
    row = lax.broadcasted_iota(jnp.int32, (BF16_SUBLANES, width), 0)
    return jnp.where(row == 0, 1.0, 0.0).astype(BF16)


def _store_vt(vt_ref, vt, n_heads):
    tk = ATTN_K_BLOCK
    ones = _ones_row_group(tk)
    for h in range(n_heads):
        for c in range(vt.shape[1] // tk):
            vt_ref[h, c, :V_HEAD, :] = vt[h * V_HEAD:(h + 1) * V_HEAD, c * tk:(c + 1) * tk].astype(BF16)
            vt_ref[h, c, V_HEAD:, :] = ones


def _mla_proj_kernel(x_ref, pos_ref, inv_ref, g_attn_ref, w_dq_ref, g_q_ref, w_uqt_ref,
                     w_ckv_ref, w_pet_ref, g_kv_ref, w_uk_ref, w_uvt_ref,
                     qt_ref, k_ref, vt_ref):
    x = x_ref[...]
    hn = ((x * _rms_scale(x)) * g_attn_ref[...]).astype(BF16)

    cq = _dot(hn, w_dq_ref[...])
    cqn = ((cq * _rms_scale(cq)) * g_q_ref[...]).astype(BF16)
    qt = _dot_nt(w_uqt_ref[...], cqn)

    c = _dot(hn, w_ckv_ref[...])
    c_kv = ((c * _rms_scale(c)) * g_kv_ref[...]).astype(BF16)
    k_nope = _dot(c_kv, w_uk_ref[...])
    vt = _dot_nt(w_uvt_ref[...], c_kv)
    pet = _dot_nt(w_pet_ref[...], hn)

    ang = inv_ref[...] * pos_ref[...]
    cos = jnp.cos(ang)
    sin = jnp.sin(ang)
    half = QK_ROPE // 2

    def rope(x1, x2):
        return x1 * cos - x2 * sin, x2 * cos + x1 * sin

    r1, r2 = rope(pet[:half], pet[half:QK_ROPE])
    k_pe = jnp.concatenate([r1, r2, pet[QK_ROPE:]], axis=0).T.astype(BF16)

    q_scale = (QK_NOPE + QK_ROPE) ** -0.5 * LOG2E
    head_dim = QK_NOPE + QK_ROPE
    zero_rows = jnp.zeros((HEAD_PAD - head_dim, x.shape[0]), BF16)
    for h in range(MLA_HEADS):
        base = h * head_dim
        qt_ref[h, :QK_NOPE, :] = (qt[base:base + QK_NOPE] * q_scale).astype(BF16)
        r1, r2 = rope(qt[base + QK_NOPE:base + QK_NOPE + half],
                      qt[base + QK_NOPE + half:base + head_dim])
        qt_ref[h, QK_NOPE:QK_NOPE + half, :] = (r1 * q_scale).astype(BF16)
        qt_ref[h, QK_NOPE + half:head_dim, :] = (r2 * q_scale).astype(BF16)
        qt_ref[h, head_dim:, :] = zero_rows
        k_ref[h, :, :LANE] = k_nope[:, h * LANE:(h + 1) * LANE].astype(BF16)
        k_ref[h, :, LANE:] = k_pe
    _store_vt(vt_ref, vt, MLA_HEADS)


def _mla_proj(x, pos_row, inv_col, g_attn, w_dq, g_q, w_uqt, w_ckv, w_pet, g_kv, w_uk, w_uvt):
    tm, tk = ROW_BLOCK, ATTN_K_BLOCK
    return pl.pallas_call(
        _mla_proj_kernel,
        grid=(SEQ // tm,),
        in_specs=[
            pl.BlockSpec((tm, D_MODEL), lambda i: (i, 0)),
            pl.BlockSpec((1, tm), lambda i: (0, i)),
            _full(inv_col.shape),
            _full((1, D_MODEL)),
            _resident(w_dq.shape),
            _full((1, Q_LORA)),
            _resident(w_uqt.shape),
            _resident(w_ckv.shape),
            _resident(w_pet.shape),
            _full((1, KV_LORA)),
            _resident(w_uk.shape),
            _resident(w_uvt.shape),
        ],
        out_specs=[
            pl.BlockSpec((MLA_HEADS, HEAD_PAD, tm), lambda i: (0, 0, i)),
            pl.BlockSpec((MLA_HEADS, tm, HEAD_PAD), lambda i: (0, i, 0)),
            pl.BlockSpec((MLA_HEADS, tm // tk, V_ROWS, tk), lambda i: (0, i, 0, 0)),
        ],
        out_shape=[
            jax.ShapeDtypeStruct((MLA_HEADS, HEAD_PAD, SEQ), BF16),
            jax.ShapeDtypeStruct((MLA_HEADS, SEQ, HEAD_PAD), BF16),
            jax.ShapeDtypeStruct((MLA_HEADS, SEQ // tk, V_ROWS, tk), BF16),
        ],
        compiler_params=pltpu.CompilerParams(
            dimension_semantics=("arbitrary",), vmem_limit_bytes=VMEM_LIMIT),
        name="mla_proj",
    )(x, pos_row, inv_col, g_attn, w_dq, g_q, w_uqt, w_ckv, w_pet, g_kv, w_uk, w_uvt)


def _online_softmax_step(s, vt_blk, m, acc_ref, p_ref=None):
    m_next = jnp.maximum(m, jnp.max(s, axis=0, keepdims=True))
    alpha = jnp.exp2(m - m_next)
    p = jnp.exp2(s - m_next).astype(BF16)
    if p_ref is not None:
        p_ref[...] = p
        p = p_ref[...]
    acc_ref[...] = alpha * acc_ref[...] + _dot(vt_blk, p)
    return m_next


def _causal_mask(s, k_offset):
    row = lax.broadcasted_iota(jnp.int32, s.shape, 0)
    col = lax.broadcasted_iota(jnp.int32, s.shape, 1)
    return jnp.where(row + k_offset <= col, s, NEG_INF)


def _normalized(acc_ref):
    return acc_ref[:V_HEAD, :] / acc_ref[V_HEAD:V_HEAD + 1, :]


def _causal_key_loop(i, scores, consume, s_refs, stats):
    group = ATTN_Q_BLOCK // ATTN_K_BLOCK

    def put(vals):
        for ref, val in zip(s_refs, vals):
            ref[...] = val

    def get():
        return tuple(ref[...] for ref in s_refs)

    put(scores(0))

    def full_group(g, stats):
        s_cur = get()
        for d in range(group):
            j = g * group + d
            s_next = scores(j + 1)
            stats = consume(s_cur, j, stats, None)
            s_cur = s_next
        put(s_cur)
        return stats

    stats = lax.fori_loop(0, i, full_group, stats)
    s_cur = get()
    for d in range(group):
        j = i * group + d
        s_next = scores(j + 1) if d + 1 < group else None
        stats = consume(s_cur, j, stats, d * ATTN_K_BLOCK)
        s_cur = s_next
    return stats


def _mla_attn_kernel(qt_ref, k_ref, vt_ref, o_ref, acc_ref, s_ref, p_ref):
    i = pl.program_id(1)
    tq, tk = ATTN_Q_BLOCK, ATTN_K_BLOCK
    acc_ref[...] = jnp.zeros(acc_ref.shape, F32)

    def scores(j):
        start = pl.multiple_of(j * tk, tk)
        return (_dot(k_ref[0, pl.ds(start, tk), :], qt_ref[0]),)

    def consume(s, j, m, k_offset):
        s = s[0] if k_offset is None else _causal_mask(s[0], k_offset)
        return _online_softmax_step(s, vt_ref[0, j], m, acc_ref, p_ref)

    _causal_key_loop(i, scores, consume, (s_ref,), jnp.full((1, tq), NEG_INF, F32))
    o_ref[...] = _normalized(acc_ref).T.astype(o_ref.dtype)


def _mla_attn(qt, k, vt):
    tq, tk = ATTN_Q_BLOCK, ATTN_K_BLOCK
    return pl.pallas_call(
        _mla_attn_kernel,
        grid=(MLA_HEADS, SEQ // tq),
        in_specs=[
            pl.BlockSpec((1, HEAD_PAD, tq), lambda h, i: (h, 0, i)),
            pl.BlockSpec((1, SEQ, HEAD_PAD), lambda h, i: (h, 0, 0)),
            pl.BlockSpec((1, SEQ // tk, V_ROWS, tk), lambda h, i: (h, 0, 0, 0)),
        ],
        out_specs=pl.BlockSpec((tq, V_HEAD), lambda h, i: (i, h)),
        out_shape=jax.ShapeDtypeStruct((SEQ, MLA_HEADS * V_HEAD), BF16),
        scratch_shapes=[pltpu.VMEM((V_ROWS, tq), F32), pltpu.VMEM((tk, tq), F32),
                        pltpu.VMEM((tk, tq), BF16)],
        compiler_params=pltpu.CompilerParams(
            dimension_semantics=("arbitrary", "arbitrary"), vmem_limit_bytes=VMEM_LIMIT),
        name="mla_attn",
    )(qt, k, vt)


def _ffn_kernel(h_ref, o_ref, w_o_ref, g_ffn_ref, w_gate_ref, w_up_ref, w_down_ref,
                g_final_ref, out_ref, *, final_norm):
    h1 = h_ref[...] + _dot(o_ref[...], w_o_ref[...])
    hn = ((h1 * _rms_scale(h1)) * g_ffn_ref[...]).astype(BF16)
    y = h1
    for c in range(D_FF // FF_CHUNK):
        g = _dot(hn, w_gate_ref[c])
        u = _dot(hn, w_up_ref[c])
        act = (g * (1.0 / (1.0 + jnp.exp(-g))) * u).astype(BF16)
        y = y + _dot(act, w_down_ref[c])
    if final_norm:
        y = (y * _rms_scale(y)) * g_final_ref[...]
    out_ref[...] = y


def _ffn(h, o, w_o, g_ffn, w_gate, w_up, w_down, g_final, *, final_norm):
    tm = ROW_BLOCK
    row = lambda i: (i, 0)
    return pl.pallas_call(
        functools.partial(_ffn_kernel, final_norm=final_norm),
        grid=(SEQ // tm,),
        in_specs=[
            pl.BlockSpec((tm, D_MODEL), row),
            pl.BlockSpec((tm, D_MODEL), row),
            _resident(w_o.shape),
            _full((1, D_MODEL)),
            _resident(w_gate.shape),
            _resident(w_up.shape),
            _resident(w_down.shape),
            _full((1, D_MODEL)),
        ],
        out_specs=pl.BlockSpec((tm, D_MODEL), row),
        out_shape=jax.ShapeDtypeStruct((SEQ, D_MODEL), F32),
        compiler_params=pltpu.CompilerParams(
            dimension_semantics=("arbitrary",), vmem_limit_bytes=VMEM_LIMIT),
        name="ffn_final" if final_norm else "ffn",
    )(h, o, w_o, g_ffn, w_gate, w_up, w_down, g_final)


def _pos_pieces(pos):
    a = pos >> POS_SHIFT_HI
    b = (pos >> POS_SHIFT_MID) & ((1 << (POS_SHIFT_HI - POS_SHIFT_MID)) - 1)
    c = pos & ((1 << POS_SHIFT_MID) - 1)
    return ((a << POS_SHIFT_HI).astype(F32), (b << POS_SHIFT_MID).astype(F32), c.astype(F32))


def _piece_pattern(idx, first, pieces):
    out = jnp.zeros(idx.shape, F32)
    for j, piece in enumerate(pieces):
        hit = (idx == first + j) | (idx == first + N_PIECES + j) | (idx == first + 2 * N_PIECES + j)
        out = jnp.where(hit, piece, out)
    return out


def _diff_proj_kernel(h_ref, pos_row_ref, pos_col_ref, g_kv_ref, g_attn_ref, w_qt_ref, w_k_ref,
                      w_vt_ref, q_const_ref, k_const_ref, qt_ref, k_ref, vt_ref):
    h = h_ref[...]
    tm = h.shape[0]
    hr = h * _rms_scale(h)
    hk = (hr * g_kv_ref[...]).astype(BF16)
    hq = (hr * g_attn_ref[...]).astype(BF16)
    qt = _dot_nt(w_qt_ref[...], hq)
    kf = _dot(hk, w_k_ref[...])
    vt = _dot_nt(w_vt_ref[...], hk)

    row = lax.broadcasted_iota(jnp.int32, (DIFF_HEAD, tm), 0)
    q_pos = _piece_pattern(row, N_BIAS, [-p for p in _pos_pieces(pos_row_ref[...])])
    lane = lax.broadcasted_iota(jnp.int32, (tm, LANE), 1)
    k_pos = _piece_pattern(lane, DIFF_HEAD, _pos_pieces(pos_col_ref[...]))

    q_scale = DIFF_HEAD ** -0.5 * LOG2E
    for hd in range(DIFF_HEADS):
        q_bias = (q_pos + q_const_ref[hd]).astype(BF16)
        k_bias = k_pos + k_const_ref[hd]
        for half in range(2):
            lo = hd * DIFF_V + half * DIFF_HEAD
            qt_ref[hd, half * LANE:half * LANE + DIFF_HEAD, :] = (
                qt[lo:lo + DIFF_HEAD] * q_scale).astype(BF16)
            qt_ref[hd, half * LANE + DIFF_HEAD:(half + 1) * LANE, :] = q_bias
            lo = hd * HEAD_PAD + half * LANE
            k_ref[hd, :, half * LANE:(half + 1) * LANE] = (kf[:, lo:lo + LANE] + k_bias).astype(BF16)
    _store_vt(vt_ref, vt, DIFF_HEADS)


def _diff_proj(h, pos_row, pos_col, g_kv, g_attn, w_qt, w_k, w_vt, q_const, k_const):
    tm, tk = ROW_BLOCK, ATTN_K_BLOCK
    return pl.pallas_call(
        _diff_proj_kernel,
        grid=(SEQ // tm,),
        in_specs=[
            pl.BlockSpec((tm, D_MODEL), lambda i: (i, 0)),
            pl.BlockSpec((1, tm), lambda i: (0, i)),
            pl.BlockSpec((tm, 1), lambda i: (i, 0)),
            _full((1, D_MODEL)),
            _full((1, D_MODEL)),
            _resident(w_qt.shape),
            _resident(w_k.shape),
            _resident(w_vt.shape),
            _full(q_const.shape),
            _full(k_const.shape),
        ],
        out_specs=[
            pl.BlockSpec((DIFF_HEADS, HEAD_PAD, tm), lambda i: (0, 0, i)),
            pl.BlockSpec((DIFF_HEADS, tm, HEAD_PAD), lambda i: (0, i, 0)),
            pl.BlockSpec((DIFF_HEADS, tm // tk, V_ROWS, tk), lambda i: (0, i, 0, 0)),
        ],
        out_shape=[
            jax.ShapeDtypeStruct((DIFF_HEADS, HEAD_PAD, SEQ), BF16),
            jax.ShapeDtypeStruct((DIFF_HEADS, SEQ, HEAD_PAD), BF16),
            jax.ShapeDtypeStruct((DIFF_HEADS, SEQ // tk, V_ROWS, tk), BF16),
        ],
        compiler_params=pltpu.CompilerParams(
            dimension_semantics=("arbitrary",), vmem_limit_bytes=VMEM_LIMIT),
        name="diff_proj",
    )(h, pos_row, pos_col, g_kv, g_attn, w_qt, w_k, w_vt, q_const, k_const)


def _diff_attn_kernel(lam_ref, qt_ref, k_ref, vt_ref, subln_ref, o_ref,
                      acc1_ref, acc2_ref, s1_ref, s2_ref):
    i = pl.program_id(1)
    tq, tk = ATTN_Q_BLOCK, ATTN_K_BLOCK
    acc1_ref[...] = jnp.zeros(acc1_ref.shape, F32)
    acc2_ref[...] = jnp.zeros(acc2_ref.shape, F32)

    def scores(j):
        start = pl.multiple_of(j * tk, tk)
        return tuple(
            _dot(k_ref[0, pl.ds(start, tk), half * LANE:(half + 1) * LANE],
                 qt_ref[0, half * LANE:(half + 1) * LANE, :]) for half in range(2))

    def consume(s, j, m, k_offset):
        vt_blk = vt_ref[0, j]
        out = []
        for s_h, m_h, acc_ref in zip(s, m, (acc1_ref, acc2_ref)):
            if k_offset is not None:
                s_h = _causal_mask(s_h, k_offset)
            out.append(_online_softmax_step(s_h, vt_blk, m_h, acc_ref))
        return tuple(out)

    m0 = jnp.full((1, tq), NEG_INF, F32)
    _causal_key_loop(i, scores, consume, (s1_ref, s2_ref), (m0, m0))

    lp = lam_ref[...]
    lam = (jnp.exp(jnp.sum(lp[0:1] * lp[1:2], axis=1, keepdims=True))
           - jnp.exp(jnp.sum(lp[2:3] * lp[3:4], axis=1, keepdims=True)) + LAMBDA_INIT)
    o = _normalized(acc1_ref) - lam * _normalized(acc2_ref)
    r = lax.rsqrt(jnp.mean(o * o, axis=0, keepdims=True) + EPS)
    o = (o * r).T * subln_ref[...] * (1.0 - LAMBDA_INIT)
    o_ref[...] = o.astype(o_ref.dtype)


def _diff_attn(lam, qt, k, vt, subln):
    tq, tk = ATTN_Q_BLOCK, ATTN_K_BLOCK
    acc = pltpu.VMEM((V_ROWS, tq), F32)
    s_tile = pltpu.VMEM((tk, tq), F32)
    return pl.pallas_call(
        _diff_attn_kernel,
        grid=(DIFF_HEADS, SEQ // tq),
        in_specs=[
            pl.BlockSpec((4, DIFF_HEAD), lambda h, i: (0, 0)),
            pl.BlockSpec((1, HEAD_PAD, tq), lambda h, i: (h, 0, i)),
            pl.BlockSpec((1, SEQ, HEAD_PAD), lambda h, i: (h, 0, 0)),
            pl.BlockSpec((1, SEQ // tk, V_ROWS, tk), lambda h, i: (h, 0, 0, 0)),
            pl.BlockSpec((1, DIFF_V), lambda h, i: (0, 0)),
        ],
        out_specs=pl.BlockSpec((tq, DIFF_V), lambda h, i: (i, h)),
        out_shape=jax.ShapeDtypeStruct((SEQ, DIFF_HEADS * DIFF_V), BF16),
        scratch_shapes=[acc, acc, s_tile, s_tile],
        compiler_params=pltpu.CompilerParams(
            dimension_semantics=("arbitrary", "arbitrary"), vmem_limit_bytes=VMEM_LIMIT),
        name="diff_attn",
    )(lam, qt, k, vt, subln)


def _pad_last(a, width):
    return jnp.pad(a, [(0, 0)] * (a.ndim - 1) + [(0, width - a.shape[-1])])


def _split_ffn(w_gate_up, w_down):
    n = D_FF // FF_CHUNK
    w_gate = w_gate_up[:, :D_FF].reshape(D_MODEL, n, FF_CHUNK).transpose(1, 0, 2)
    w_up = w_gate_up[:, D_FF:].reshape(D_MODEL, n, FF_CHUNK).transpose(1, 0, 2)
    return w_gate.astype(BF16), w_up.astype(BF16), w_down.reshape(n, FF_CHUNK, D_MODEL).astype(BF16)


def _alibi_constants():
    slopes = 2.0 ** (-8.0 * jnp.arange(1, DIFF_HEADS + 1, dtype=F32) / DIFF_HEADS)
    c = slopes * LOG2E
    pieces = []
    for _ in range(N_PIECES):
        piece = c.astype(BF16).astype(F32)
        pieces.append(piece)
        c = c - piece
    per_dim = jnp.repeat(jnp.stack(pieces, axis=1), N_PIECES, axis=1)
    q_const = jnp.pad(per_dim, ((0, 0), (0, DIFF_HEAD - N_BIAS)))[:, :, None]
    k_const = jnp.pad(per_dim, ((0, 0), (DIFF_HEAD + N_BIAS, LANE - DIFF_HEAD - 2 * N_BIAS)))
    return q_const, k_const[:, None, :]


def kernel(x, positions, attn_norm, ffn_norm, final_norm, mla_w_dq, mla_q_norm, mla_w_uq,
           mla_w_dkv, mla_kv_norm, mla_w_ukv, mla_w_o, diff_kv_norm, diff_w_k, diff_w_v,
           diff_w_q, diff_lambda_q1, diff_lambda_k1, diff_lambda_q2, diff_lambda_k2,
           diff_subln, diff_w_o, ffn_w_gate_up, ffn_w_down):
    x2 = x.reshape(SEQ, D_MODEL)
    pos_row = positions.reshape(1, SEQ)
    pos_col = positions.reshape(SEQ, 1)
    row_vec = lambda a: a.reshape(1, -1)

    half = QK_ROPE // 2
    inv_col = (ROPE_THETA ** (-jnp.arange(half, dtype=F32) * 2.0 / QK_ROPE)).reshape(half, 1)

    w_dkv = mla_w_dkv[0]
    w_pet = jnp.pad(w_dkv[:, KV_LORA:].T, ((0, LANE - QK_ROPE), (0, 0))).astype(BF16)
    w_ukv = mla_w_ukv[0].reshape(KV_LORA, MLA_HEADS, QK_NOPE + V_HEAD)
    w_uk = w_ukv[:, :, :QK_NOPE].reshape(KV_LORA, MLA_HEADS * QK_NOPE).astype(BF16)
    w_uvt = w_ukv[:, :, QK_NOPE:].reshape(KV_LORA, MLA_HEADS * V_HEAD).T.astype(BF16)
    qt, k, vt = _mla_proj(
        x2, pos_row.astype(F32), inv_col, row_vec(attn_norm[0]), mla_w_dq[0].astype(BF16),
        row_vec(mla_q_norm[0]), mla_w_uq[0].T.astype(BF16), w_dkv[:, :KV_LORA].astype(BF16),
        w_pet, row_vec(mla_kv_norm[0]), w_uk, w_uvt)
    o = _mla_attn(qt, k, vt)
    w_gate, w_up, w_down = _split_ffn(ffn_w_gate_up[0], ffn_w_down[0])
    h = _ffn(x2, o, mla_w_o[0].astype(BF16), row_vec(ffn_norm[0]), w_gate, w_up, w_down,
             row_vec(final_norm), final_norm=False)

    w_k = diff_w_k.reshape(D_MODEL, DIFF_HEADS, 2, DIFF_HEAD)
    w_k = _pad_last(w_k, LANE).reshape(D_MODEL, DIFF_HEADS * HEAD_PAD).astype(BF16)
    q_const, k_const = _alibi_constants()
    qdt, kd, vdt = _diff_proj(h, pos_row, pos_col, row_vec(diff_kv_norm), row_vec(attn_norm[1]),
                              diff_w_q[0].T.astype(BF16), w_k, diff_w_v.T.astype(BF16),
                              q_const, k_const)
    lam = jnp.stack([diff_lambda_q1[0], diff_lambda_k1[0], diff_lambda_q2[0], diff_lambda_k2[0]])
    od = _diff_attn(lam, qdt, kd, vdt, row_vec(diff_subln[0]))
    w_gate, w_up, w_down = _split_ffn(ffn_w_gate_up[1], ffn_w_down[1])
    out = _ffn(h, od, diff_w_o[0].astype(BF16), row_vec(ffn_norm[1]), w_gate, w_up, w_down,
               row_vec(final_norm), final_norm=True)
    return out.reshape(x.shape)
```

```python
import functools
import math

import jax
import jax.numpy as jnp
import numpy as np
from jax import lax
from jax.experimental import pallas as pl
from jax.experimental.pallas import tpu as pltpu

D_MODEL = 1024
SEQ = 16384
MLA_HEADS = 8
QK_NOPE = 128
QK_ROPE = 64
V_HEAD = 128
Q_LORA = 512
KV_LORA = 256
ROPE_THETA = 10000.0
DIFF_HEADS = 8
DIFF_HEAD = 64
DIFF_V = 2 * DIFF_HEAD
D_FF = 2816
EPS = 1e-6
NEG_INF = -1e30
LAMBDA_INIT = 0.8 - 0.6 * float(np.exp(-0.3 * 1))

LANE = 128
HEAD_PAD = 2 * LANE
LOG2E = math.log2(math.e)
VMEM_LIMIT = 56 * 1024 * 1024

ROW_BLOCK = 512
ATTN_Q_BLOCK = 1024
ATTN_K_BLOCK = 256
ATTN_STREAM_LANES = 256
FF_CHUNK = D_FF // 2
BF16_SUBLANES = 16
V_ROWS = V_HEAD + BF16_SUBLANES

POS_SHIFT_HI = 14
POS_SHIFT_MID = 7
N_PIECES = 3
N_BIAS = N_PIECES * N_PIECES

BF16 = jnp.bfloat16
F32 = jnp.float32


def _dot(a, b):
    return jnp.dot(a, b, preferred_element_type=F32)


def _dot_nt(a, b):
    return lax.dot_general(a, b, (((1,), (1,)), ((), ())), preferred_element_type=F32)


def _rms_scale(x):
    return lax.rsqrt(jnp.mean(x * x, axis=-1, keepdims=True) + EPS)


def _full(shape):
    return pl.BlockSpec(shape, lambda *_: (0,) * len(shape))


def _resident(shape):
    return pl.BlockSpec(shape, lambda *_: (0,) * len(shape), pipeline_mode=pl.Buffered(1))


def _ones_row_group(width):
    row = lax.broadcasted_iota(jnp.int32, (BF16_SUBLANES, width), 0)
    return jnp.where(row == 0, 1.0, 0.0).astype(BF16)


def _store_vt(vt_ref, vt, n_heads):
    tk = ATTN_K_BLOCK
    ones = _ones_row_group(tk)
    for h in range(n_heads):
        for c in range(vt.shape[1] // tk):
            vt_ref[h, c, :V_HEAD, :] = vt[h * V_HEAD:(h + 1) * V_HEAD, c * tk:(c + 1) * tk].astype(BF16)
            vt_ref[h, c, V_HEAD:, :] = ones


def _mla_proj_kernel(x_ref, pos_ref, inv_ref, g_attn_ref, w_dq_ref, g_q_ref, w_uqt_ref,
                     w_ckv_ref, w_pet_ref, g_kv_ref, w_uk_ref, w_uvt_ref,
                     qt_ref, k_ref, vt_ref):
    x = x_ref[...]
    hn = ((x * _rms_scale(x)) * g_attn_ref[...]).astype(BF16)

    cq = _dot(hn, w_dq_ref[...])
    cqn = ((cq * _rms_scale(cq)) * g_q_ref[...]).astype(BF16)
    qt = _dot_nt(w_uqt_ref[...], cqn)

    c = _dot(hn, w_ckv_ref[...])
    c_kv = ((c * _rms_scale(c)) * g_kv_ref[...]).astype(BF16)
    k_nope = _dot(c_kv, w_uk_ref[...])
    vt = _dot_nt(w_uvt_ref[...], c_kv)
    pet = _dot_nt(w_pet_ref[...], hn)

    ang = inv_ref[...] * pos_ref[...]
    cos = jnp.cos(ang)
    sin = jnp.sin(ang)
    half = QK_ROPE // 2

    def rope(x1, x2):
        return x1 * cos - x2 * sin, x2 * cos + x1 * sin

    r1, r2 = rope(pet[:half], pet[half:QK_ROPE])
    k_pe = jnp.concatenate([r1, r2, pet[QK_ROPE:]], axis=0).T.astype(BF16)

    q_scale = (QK_NOPE + QK_ROPE) ** -0.5 * LOG2E
    head_dim = QK_NOPE + QK_ROPE
    zero_rows = jnp.zeros((HEAD_PAD - head_dim, x.shape[0]), BF16)
    for h in range(MLA_HEADS):
        base = h * head_dim
        qt_ref[h, :QK_NOPE, :] = (qt[base:base + QK_NOPE] * q_scale).astype(BF16)
        r1, r2 = rope(qt[base + QK_NOPE:base + QK_NOPE + half],
                      qt[base + QK_NOPE + half:base + head_dim])
        qt_ref[h, QK_NOPE:QK_NOPE + half, :] = (r1 * q_scale).astype(BF16)
        qt_ref[h, QK_NOPE + half:head_dim, :] = (r2 * q_scale).astype(BF16)
        qt_ref[h, head_dim:, :] = zero_rows
        k_ref[h, :, :LANE] = k_nope[:, h * LANE:(h + 1) * LANE].astype(BF16)
        k_ref[h, :, LANE:] = k_pe
    _store_vt(vt_ref, vt, MLA_HEADS)


def _mla_proj(x, pos_row, inv_col, g_attn, w_dq, g_q, w_uqt, w_ckv, w_pet, g_kv, w_uk, w_uvt):
    tm, tk = ROW_BLOCK, ATTN_K_BLOCK
    return pl.pallas_call(
        _mla_proj_kernel,
        grid=(SEQ // tm,),
        in_specs=[
            pl.BlockSpec((tm, D_MODEL), lambda i: (i, 0)),
            pl.BlockSpec((1, tm), lambda i: (0, i)),
            _full(inv_col.shape),
            _full((1, D_MODEL)),
            _resident(w_dq.shape),
            _full((1, Q_LORA)),
            _resident(w_uqt.shape),
            _resident(w_ckv.shape),
            _resident(w_pet.shape),
            _full((1, KV_LORA)),
            _resident(w_uk.shape),
            _resident(w_uvt.shape),
        ],
        out_specs=[
            pl.BlockSpec((MLA_HEADS, HEAD_PAD, tm), lambda i: (0, 0, i)),
            pl.BlockSpec((MLA_HEADS, tm, HEAD_PAD), lambda i: (0, i, 0)),
            pl.BlockSpec((MLA_HEADS, tm // tk, V_ROWS, tk), lambda i: (0, i, 0, 0)),
        ],
        out_shape=[
            jax.ShapeDtypeStruct((MLA_HEADS, HEAD_PAD, SEQ), BF16),
            jax.ShapeDtypeStruct((MLA_HEADS, SEQ, HEAD_PAD), BF16),
            jax.ShapeDtypeStruct((MLA_HEADS, SEQ // tk, V_ROWS, tk), BF16),
        ],
        compiler_params=pltpu.CompilerParams(
            dimension_semantics=("arbitrary",), vmem_limit_bytes=VMEM_LIMIT),
        name="mla_proj",
    )(x, pos_row, inv_col, g_attn, w_dq, g_q, w_uqt, w_ckv, w_pet, g_kv, w_uk, w_uvt)


def _online_softmax_step(s, vt_blk, m, acc_ref, base, lo, width):
    m_old = m[:, lo:]
    m_next = jnp.maximum(m_old, jnp.max(s, axis=0, keepdims=True))
    alpha = jnp.exp2(m_old - m_next)
    p = jnp.exp2(s - m_next).astype(BF16)
    lanes = slice(base + lo, base + width)
    acc_ref[:, lanes] = alpha * acc_ref[:, lanes] + _dot(vt_blk, p)
    return m_next if lo == 0 else jnp.concatenate([m[:, :lo], m_next], axis=1)


def _stream(qt_ref, k_ref, vt_ref, acc_ref, s_ref, dims, base, width):
    tk = ATTN_K_BLOCK

    def scores(j, lo):
        start = pl.multiple_of(j * tk, tk)
        return _dot(k_ref[0, pl.ds(start, tk), dims], qt_ref[0, dims, base + lo:base + width])

    def consume(s, j, m, lo):
        return _online_softmax_step(s, vt_ref[0, j], m, acc_ref, base, lo, width)

    return scores, consume, s_ref, base, width


def _diagonal_mask(s):
    row = lax.broadcasted_iota(jnp.int32, s.shape, 0)
    col = lax.broadcasted_iota(jnp.int32, s.shape, 1)
    return jnp.where(row <= col, s, NEG_INF)


def _normalized(acc_ref):
    return acc_ref[:V_HEAD, :] / acc_ref[V_HEAD:V_HEAD + 1, :]


def _causal_key_loop(i, streams):
    tk = ATTN_K_BLOCK
    group = ATTN_Q_BLOCK // tk
    for scores, _, s_ref, _, _ in streams:
        s_ref[...] = scores(0, 0)

    def full_group(g, m):
        m = list(m)
        s_cur = [st[2][...] for st in streams]
        for d in range(group):
            j = g * group + d
            for n, (scores, consume, _, _, _) in enumerate(streams):
                s_next = scores(j + 1, 0)
                m[n] = consume(s_cur[n], j, m[n], 0)
                s_cur[n] = s_next
        for st, s in zip(streams, s_cur):
            st[2][...] = s
        return tuple(m)

    def diagonal_plan(d, base, width):
        first = d * tk - base
        if first >= width:
            return None
        return (first, True) if first >= 0 else (0, False)

    m0 = tuple(jnp.full((1, st[4]), NEG_INF, F32) for st in streams)
    m = list(lax.fori_loop(0, i, full_group, m0))
    s_cur = [st[2][...] for st in streams]
    for d in range(group):
        j = i * group + d
        for n, (scores, consume, _, base, width) in enumerate(streams):
            plan = diagonal_plan(d, base, width)
            ahead = diagonal_plan(d + 1, base, width) if d + 1 < group else None
            s_next = scores(j + 1, ahead[0]) if ahead else None
            if plan:
                s = _diagonal_mask(s_cur[n]) if plan[1] else s_cur[n]
                m[n] = consume(s, j, m[n], plan[0])
            s_cur[n] = s_next


def _mla_attn_kernel(qt_ref, k_ref, vt_ref, o_ref, acc_ref, *s_refs):
    acc_ref[...] = jnp.zeros(acc_ref.shape, F32)
    w = ATTN_STREAM_LANES
    streams = [_stream(qt_ref, k_ref, vt_ref, acc_ref, s_ref, slice(None), n * w, w)
               for n, s_ref in enumerate(s_refs)]
    _causal_key_loop(pl.program_id(1), streams)
    o_ref[...] = _normalized(acc_ref).T.astype(o_ref.dtype)


def _mla_attn(qt, k, vt):
    tq, tk = ATTN_Q_BLOCK, ATTN_K_BLOCK
    return pl.pallas_call(
        _mla_attn_kernel,
        grid=(MLA_HEADS, SEQ // tq),
        in_specs=[
            pl.BlockSpec((1, HEAD_PAD, tq), lambda h, i: (h, 0, i)),
            pl.BlockSpec((1, SEQ, HEAD_PAD), lambda h, i: (h, 0, 0)),
            pl.BlockSpec((1, SEQ // tk, V_ROWS, tk), lambda h, i: (h, 0, 0, 0)),
        ],
        out_specs=pl.BlockSpec((tq, V_HEAD), lambda h, i: (i, h)),
        out_shape=jax.ShapeDtypeStruct((SEQ, MLA_HEADS * V_HEAD), BF16),
        scratch_shapes=[pltpu.VMEM((V_ROWS, tq), F32)]
        + [pltpu.VMEM((tk, ATTN_STREAM_LANES), F32)] * (tq // ATTN_STREAM_LANES),
        compiler_params=pltpu.CompilerParams(
            dimension_semantics=("arbitrary", "arbitrary"), vmem_limit_bytes=VMEM_LIMIT),
        name="mla_attn",
    )(qt, k, vt)


def _ffn_kernel(h_ref, o_ref, w_o_ref, g_ffn_ref, w_gate_ref, w_up_ref, w_down_ref,
                g_final_ref, out_ref, *, final_norm):
    h1 = h_ref[...] + _dot(o_ref[...], w_o_ref[...])
    hn = ((h1 * _rms_scale(h1)) * g_ffn_ref[...]).astype(BF16)
    y = h1
    for c in range(D_FF // FF_CHUNK):
        g = _dot(hn, w_gate_ref[c])
        u = _dot(hn, w_up_ref[c])
        act = (g * (1.0 / (1.0 + jnp.exp(-g))) * u).astype(BF16)
        y = y + _dot(act, w_down_ref[c])
    if final_norm:
        y = (y * _rms_scale(y)) * g_final_ref[...]
    out_ref[...] = y


def _ffn(h, o, w_o, g_ffn, w_gate, w_up, w_down, g_final, *, final_norm):
    tm = ROW_BLOCK
    row = lambda i: (i, 0)
    return pl.pallas_call(
        functools.partial(_ffn_kernel, final_norm=final_norm),
        grid=(SEQ // tm,),
        in_specs=[
            pl.BlockSpec((tm, D_MODEL), row),
            pl.BlockSpec((tm, D_MODEL), row),
            _resident(w_o.shape),
            _full((1, D_MODEL)),
            _resident(w_gate.shape),
            _resident(w_up.shape),
            _resident(w_down.shape),
            _full((1, D_MODEL)),
        ],
        out_specs=pl.BlockSpec((tm, D_MODEL), row),
        out_shape=jax.ShapeDtypeStruct((SEQ, D_MODEL), F32),
        compiler_params=pltpu.CompilerParams(
            dimension_semantics=("arbitrary",), vmem_limit_bytes=VMEM_LIMIT),
        name="ffn_final" if final_norm else "ffn",
    )(h, o, w_o, g_ffn, w_gate, w_up, w_down, g_final)


def _pos_pieces(pos):
    a = pos >> POS_SHIFT_HI
    b = (pos >> POS_SHIFT_MID) & ((1 << (POS_SHIFT_HI - POS_SHIFT_MID)) - 1)
    c = pos & ((1 << POS_SHIFT_MID) - 1)
    return ((a << POS_SHIFT_HI).astype(F32), (b << POS_SHIFT_MID).astype(F32), c.astype(F32))


def _piece_pattern(idx, first, pieces):
    out = jnp.zeros(idx.shape, F32)
    for j, piece in enumerate(pieces):
        hit = (idx == first + j) | (idx == first + N_PIECES + j) | (idx == first + 2 * N_PIECES + j)
        out = jnp.where(hit, piece, out)
    return out


def _diff_proj_kernel(h_ref, pos_row_ref, pos_col_ref, g_kv_ref, g_attn_ref, w_qt_ref, w_k_ref,
                      w_vt_ref, q_const_ref, k_const_ref, qt_ref, k_ref, vt_ref):
    h = h_ref[...]
    tm = h.shape[0]
    hr = h * _rms_scale(h)
    hk = (hr * g_kv_ref[...]).astype(BF16)
    hq = (hr * g_attn_ref[...]).astype(BF16)
    qt = _dot_nt(w_qt_ref[...], hq)
    kf = _dot(hk, w_k_ref[...])
    vt = _dot_nt(w_vt_ref[...], hk)

    row = lax.broadcasted_iota(jnp.int32, (DIFF_HEAD, tm), 0)
    q_pos = _piece_pattern(row, N_BIAS, [-p for p in _pos_pieces(pos_row_ref[...])])
    lane = lax.broadcasted_iota(jnp.int32, (tm, LANE), 1)
    k_pos = _piece_pattern(lane, DIFF_HEAD, _pos_pieces(pos_col_ref[...]))

    q_scale = DIFF_HEAD ** -0.5 * LOG2E
    for hd in range(DIFF_HEADS):
        q_bias = (q_pos + q_const_ref[hd]).astype(BF16)
        k_bias = k_pos + k_const_ref[hd]
        k12 = kf[:, hd * DIFF_V:(hd + 1) * DIFF_V]
        k_halves = (k12, pltpu.roll(k12, DIFF_HEAD, 1))
        for half in range(2):
            lo = hd * DIFF_V + half * DIFF_HEAD
            qt_ref[hd, half * LANE:half * LANE + DIFF_HEAD, :] = (
                qt[lo:lo + DIFF_HEAD] * q_scale).astype(BF16)
            qt_ref[hd, half * LANE + DIFF_HEAD:(half + 1) * LANE, :] = q_bias
            k_ref[hd, :, half * LANE:(half + 1) * LANE] = jnp.where(
                lane < DIFF_HEAD, k_halves[half], k_bias).astype(BF16)
    _store_vt(vt_ref, vt, DIFF_HEADS)


def _diff_proj(h, pos_row, pos_col, g_kv, g_attn, w_qt, w_k, w_vt, q_const, k_const):
    tm, tk = ROW_BLOCK, ATTN_K_BLOCK
    return pl.pallas_call(
        _diff_proj_kernel,
        grid=(SEQ // tm,),
        in_specs=[
            pl.BlockSpec((tm, D_MODEL), lambda i: (i, 0)),
            pl.BlockSpec((1, tm), lambda i: (0, i)),
            pl.BlockSpec((tm, 1), lambda i: (i, 0)),
            _full((1, D_MODEL)),
            _full((1, D_MODEL)),
            _resident(w_qt.shape),
            _resident(w_k.shape),
            _resident(w_vt.shape),
            _full(q_const.shape),
            _full(k_const.shape),
        ],
        out_specs=[
            pl.BlockSpec((DIFF_HEADS, HEAD_PAD, tm), lambda i: (0, 0, i)),
            pl.BlockSpec((DIFF_HEADS, tm, HEAD_PAD), lambda i: (0, i, 0)),
            pl.BlockSpec((DIFF_HEADS, tm // tk, V_ROWS, tk), lambda i: (0, i, 0, 0)),
        ],
        out_shape=[
            jax.ShapeDtypeStruct((DIFF_HEADS, HEAD_PAD, SEQ), BF16),
            jax.ShapeDtypeStruct((DIFF_HEADS, SEQ, HEAD_PAD), BF16),
            jax.ShapeDtypeStruct((DIFF_HEADS, SEQ // tk, V_ROWS, tk), BF16),
        ],
        compiler_params=pltpu.CompilerParams(
            dimension_semantics=("arbitrary",), vmem_limit_bytes=VMEM_LIMIT),
        name="diff_proj",
    )(h, pos_row, pos_col, g_kv, g_attn, w_qt, w_k, w_vt, q_const, k_const)


def _diff_attn_kernel(lam_ref, qt_ref, k_ref, vt_ref, subln_ref, o_ref,
                      acc1_ref, acc2_ref, *s_refs):
    acc1_ref[...] = jnp.zeros(acc1_ref.shape, F32)
    acc2_ref[...] = jnp.zeros(acc2_ref.shape, F32)
    w = ATTN_STREAM_LANES
    per_half = len(s_refs) // 2
    streams = [
        _stream(qt_ref, k_ref, vt_ref, (acc1_ref, acc2_ref)[half], s_refs[n * 2 + half],
                slice(half * LANE, (half + 1) * LANE), n * w, w)
        for n in range(per_half) for half in range(2)]
    _causal_key_loop(pl.program_id(1), streams)

    lp = lam_ref[...]
    lam = (jnp.exp(jnp.sum(lp[0:1] * lp[1:2], axis=1, keepdims=True))
           - jnp.exp(jnp.sum(lp[2:3] * lp[3:4], axis=1, keepdims=True)) + LAMBDA_INIT)
    o = _normalized(acc1_ref) - lam * _normalized(acc2_ref)
    r = lax.rsqrt(jnp.mean(o * o, axis=0, keepdims=True) + EPS)
    o = (o * r).T * subln_ref[...] * (1.0 - LAMBDA_INIT)
    o_ref[...] = o.astype(o_ref.dtype)


def _diff_attn(lam, qt, k, vt, subln):
    tq, tk = ATTN_Q_BLOCK, ATTN_K_BLOCK
    acc = pltpu.VMEM((V_ROWS, tq), F32)
    s_tile = pltpu.VMEM((tk, ATTN_STREAM_LANES), F32)
    return pl.pallas_call(
        _diff_attn_kernel,
        grid=(DIFF_HEADS, SEQ // tq),
        in_specs=[
            pl.BlockSpec((4, DIFF_HEAD), lambda h, i: (0, 0)),
            pl.BlockSpec((1, HEAD_PAD, tq), lambda h, i: (h, 0, i)),
            pl.BlockSpec((1, SEQ, HEAD_PAD), lambda h, i: (h, 0, 0)),
            pl.BlockSpec((1, SEQ // tk, V_ROWS, tk), lambda h, i: (h, 0, 0, 0)),
            pl.BlockSpec((1, DIFF_V), lambda h, i: (0, 0)),
        ],
        out_specs=pl.BlockSpec((tq, DIFF_V), lambda h, i: (i, h)),
        out_shape=jax.ShapeDtypeStruct((SEQ, DIFF_HEADS * DIFF_V), BF16),
        scratch_shapes=[acc, acc] + [s_tile] * (2 * tq // ATTN_STREAM_LANES),
        compiler_params=pltpu.CompilerParams(
            dimension_semantics=("arbitrary", "arbitrary"), vmem_limit_bytes=VMEM_LIMIT),
        name="diff_attn",
    )(lam, qt, k, vt, subln)


def _split_ffn(w_gate_up, w_down):
    n = D_FF // FF_CHUNK
    w_gate = w_gate_up[:, :D_FF].reshape(D_MODEL, n, FF_CHUNK).transpose(1, 0, 2)
    w_up = w_gate_up[:, D_FF:].reshape(D_MODEL, n, FF_CHUNK).transpose(1, 0, 2)
    return w_gate.astype(BF16), w_up.astype(BF16), w_down.reshape(n, FF_CHUNK, D_MODEL).astype(BF16)


def _alibi_constants():
    slopes = 2.0 ** (-8.0 * jnp.arange(1, DIFF_HEADS + 1, dtype=F32) / DIFF_HEADS)
    c = slopes * LOG2E
    pieces = []
    for _ in range(N_PIECES):
        piece = c.astype(BF16).astype(F32)
        pieces.append(piece)
        c = c - piece
    per_dim = jnp.repeat(jnp.stack(pieces, axis=1), N_PIECES, axis=1)
    q_const = jnp.pad(per_dim, ((0, 0), (0, DIFF_HEAD - N_BIAS)))[:, :, None]
    k_const = jnp.pad(per_dim, ((0, 0), (DIFF_HEAD + N_BIAS, LANE - DIFF_HEAD - 2 * N_BIAS)))
    return q_const, k_const[:, None, :]


def kernel(x, positions, attn_norm, ffn_norm, final_norm, mla_w_dq, mla_q_norm, mla_w_uq,
           mla_w_dkv, mla_kv_norm, mla_w_ukv, mla_w_o, diff_kv_norm, diff_w_k, diff_w_v,
           diff_w_q, diff_lambda_q1, diff_lambda_k1, diff_lambda_q2, diff_lambda_k2,
           diff_subln, diff_w_o, ffn_w_gate_up, ffn_w_down):
    x2 = x.reshape(SEQ, D_MODEL)
    pos_row = positions.reshape(1, SEQ)
    pos_col = positions.reshape(SEQ, 1)
    row_vec = lambda a: a.reshape(1, -1)

    half = QK_ROPE // 2
    inv_col = (ROPE_THETA ** (-jnp.arange(half, dtype=F32) * 2.0 / QK_ROPE)).reshape(half, 1)

    w_dkv = mla_w_dkv[0]
    w_pet = jnp.pad(w_dkv[:, KV_LORA:].T, ((0, LANE - QK_ROPE), (0, 0))).astype(BF16)
    w_ukv = mla_w_ukv[0].reshape(KV_LORA, MLA_HEADS, QK_NOPE + V_HEAD)
    w_uk = w_ukv[:, :, :QK_NOPE].reshape(KV_LORA, MLA_HEADS * QK_NOPE).astype(BF16)
    w_uvt = w_ukv[:, :, QK_NOPE:].reshape(KV_LORA, MLA_HEADS * V_HEAD).T.astype(BF16)
    qt, k, vt = _mla_proj(
        x2, pos_row.astype(F32), inv_col, row_vec(attn_norm[0]), mla_w_dq[0].astype(BF16),
        row_vec(mla_q_norm[0]), mla_w_uq[0].T.astype(BF16), w_dkv[:, :KV_LORA].astype(BF16),
        w_pet, row_vec(mla_kv_norm[0]), w_uk, w_uvt)
    o = _mla_attn(qt, k, vt)
    w_gate, w_up, w_down = _split_ffn(ffn_w_gate_up[0], ffn_w_down[0])
    h = _ffn(x2, o, mla_w_o[0].astype(BF16), row_vec(ffn_norm[0]), w_gate, w_up, w_down,
             row_vec(final_norm), final_norm=False)

    q_const, k_const = _alibi_constants()
    qdt, kd, vdt = _diff_proj(h, pos_row, pos_col, row_vec(diff_kv_norm), row_vec(attn_norm[1]),
                              diff_w_q[0].T.astype(BF16), diff_w_k.astype(BF16),
                              diff_w_v.T.astype(BF16), q_const, k_const)
    lam = jnp.stack([diff_lambda_q1[0], diff_lambda_k1[0], diff_lambda_q2[0], diff_lambda_k2[0]])
    od = _diff_attn(lam, qdt, kd, vdt, row_vec(diff_subln[0]))
    w_gate, w_up, w_down = _split_ffn(ffn_w_gate_up[1], ffn_w_down[1])
    out = _ffn(h, od, diff_w_o[0].astype(BF16), row_vec(ffn_norm[1]), w_gate, w_up, w_down,
               row_vec(final_norm), final_norm=True)
    return out.reshape(x.shape)
```

```python
import functools
import math

import jax
import jax.numpy as jnp
import numpy as np
from jax import lax
from jax.experimental import pallas as pl
from jax.experimental.pallas import tpu as pltpu

D_MODEL = 1024
SEQ = 16384
MLA_HEADS = 8
QK_NOPE = 128
QK_ROPE = 64
V_HEAD = 128
Q_LORA = 512
KV_LORA = 256
ROPE_THETA = 10000.0
DIFF_HEADS = 8
DIFF_HEAD = 64
DIFF_V = 2 * DIFF_HEAD
D_FF = 2816
EPS = 1e-6
NEG_INF = -1e30
LAMBDA_INIT = 0.8 - 0.6 * float(np.exp(-0.3 * 1))

LANE = 128
HEAD_PAD = 2 * LANE
LOG2E = math.log2(math.e)
VMEM_LIMIT = 56 * 1024 * 1024

ROW_BLOCK = 512
MLA_Q_BLOCK = 2048
DIFF_Q_BLOCK = 2048
ATTN_K_BLOCK = 256
ATTN_STREAM_LANES = 256
FF_CHUNK = D_FF // 2
BF16_SUBLANES = 16
V_ROWS = V_HEAD + BF16_SUBLANES

POS_SHIFT_HI = 14
POS_SHIFT_MID = 7
N_PIECES = 3
N_BIAS = N_PIECES * N_PIECES

BF16 = jnp.bfloat16
F32 = jnp.float32


def _dot(a, b):
    return jnp.dot(a, b, preferred_element_type=F32)


def _dot_nt(a, b):
    return lax.dot_general(a, b, (((1,), (1,)), ((), ())), preferred_element_type=F32)


def _rms_scale(x):
    return lax.rsqrt(jnp.mean(x * x, axis=-1, keepdims=True) + EPS)


def _full(shape):
    return pl.BlockSpec(shape, lambda *_: (0,) * len(shape))


def _resident(shape):
    return pl.BlockSpec(shape, lambda *_: (0,) * len(shape), pipeline_mode=pl.Buffered(1))


def _ones_row_group(width):
    row = lax.broadcasted_iota(jnp.int32, (BF16_SUBLANES, width), 0)
    return jnp.where(row == 0, 1.0, 0.0).astype(BF16)


def _store_vt(vt_ref, vt, n_heads):
    tk = ATTN_K_BLOCK
    ones = _ones_row_group(tk)
    for h in range(n_heads):
        for c in range(vt.shape[1] // tk):
            vt_ref[h, c, :V_HEAD, :] = vt[h * V_HEAD:(h + 1) * V_HEAD, c * tk:(c + 1) * tk].astype(BF16)
            vt_ref[h, c, V_HEAD:, :] = ones


def _mla_proj_kernel(x_ref, pos_ref, inv_ref, g_attn_ref, w_dq_ref, g_q_ref, w_uqt_ref,
                     w_ckv_ref, w_pet_ref, g_kv_ref, w_uk_ref, w_uvt_ref,
                     qt_ref, k_ref, vt_ref):
    x = x_ref[...]
    hn = ((x * _rms_scale(x)) * g_attn_ref[...]).astype(BF16)

    cq = _dot(hn, w_dq_ref[...])
    cqn = ((cq * _rms_scale(cq)) * g_q_ref[...]).astype(BF16)
    qt = _dot_nt(w_uqt_ref[...], cqn)

    c = _dot(hn, w_ckv_ref[...])
    c_kv = ((c * _rms_scale(c)) * g_kv_ref[...]).astype(BF16)
    k_nope = _dot(c_kv, w_uk_ref[...])
    vt = _dot_nt(w_uvt_ref[...], c_kv)
    pet = _dot_nt(w_pet_ref[...], hn)

    ang = inv_ref[...] * pos_ref[...]
    cos = jnp.cos(ang)
    sin = jnp.sin(ang)
    half = QK_ROPE // 2

    def rope(x1, x2):
        return x1 * cos - x2 * sin, x2 * cos + x1 * sin

    r1, r2 = rope(pet[:half], pet[half:QK_ROPE])
    k_pe = jnp.concatenate([r1, r2, pet[QK_ROPE:]], axis=0).T.astype(BF16)

    q_scale = (QK_NOPE + QK_ROPE) ** -0.5 * LOG2E
    head_dim = QK_NOPE + QK_ROPE
    zero_rows = jnp.zeros((HEAD_PAD - head_dim, x.shape[0]), BF16)
    for h in range(MLA_HEADS):
        base = h * head_dim
        qt_ref[h, :QK_NOPE, :] = (qt[base:base + QK_NOPE] * q_scale).astype(BF16)
        r1, r2 = rope(qt[base + QK_NOPE:base + QK_NOPE + half],
                      qt[base + QK_NOPE + half:base + head_dim])
        qt_ref[h, QK_NOPE:QK_NOPE + half, :] = (r1 * q_scale).astype(BF16)
        qt_ref[h, QK_NOPE + half:head_dim, :] = (r2 * q_scale).astype(BF16)
        qt_ref[h, head_dim:, :] = zero_rows
        k_ref[h, :, :LANE] = k_nope[:, h * LANE:(h + 1) * LANE].astype(BF16)
        k_ref[h, :, LANE:] = k_pe
    _store_vt(vt_ref, vt, MLA_HEADS)


def _mla_proj(x, pos_row, inv_col, g_attn, w_dq, g_q, w_uqt, w_ckv, w_pet, g_kv, w_uk, w_uvt):
    tm, tk = ROW_BLOCK, ATTN_K_BLOCK
    return pl.pallas_call(
        _mla_proj_kernel,
        grid=(SEQ // tm,),
        in_specs=[
            pl.BlockSpec((tm, D_MODEL), lambda i: (i, 0)),
            pl.BlockSpec((1, tm), lambda i: (0, i)),
            _full(inv_col.shape),
            _full((1, D_MODEL)),
            _resident(w_dq.shape),
            _full((1, Q_LORA)),
            _resident(w_uqt.shape),
            _resident(w_ckv.shape),
            _resident(w_pet.shape),
            _full((1, KV_LORA)),
            _resident(w_uk.shape),
            _resident(w_uvt.shape),
        ],
        out_specs=[
            pl.BlockSpec((MLA_HEADS, HEAD_PAD, tm), lambda i: (0, 0, i)),
            pl.BlockSpec((MLA_HEADS, tm, HEAD_PAD), lambda i: (0, i, 0)),
            pl.BlockSpec((MLA_HEADS, tm // tk, V_ROWS, tk), lambda i: (0, i, 0, 0)),
        ],
        out_shape=[
            jax.ShapeDtypeStruct((MLA_HEADS, HEAD_PAD, SEQ), BF16),
            jax.ShapeDtypeStruct((MLA_HEADS, SEQ, HEAD_PAD), BF16),
            jax.ShapeDtypeStruct((MLA_HEADS, SEQ // tk, V_ROWS, tk), BF16),
        ],
        compiler_params=pltpu.CompilerParams(
            dimension_semantics=("arbitrary",), vmem_limit_bytes=VMEM_LIMIT),
        name="mla_proj",
    )(x, pos_row, inv_col, g_attn, w_dq, g_q, w_uqt, w_ckv, w_pet, g_kv, w_uk, w_uvt)


def _online_softmax_step(s, vt_blk, m, acc_ref, base, lo, width):
    m_old = m[:, lo:]
    m_next = jnp.maximum(m_old, jnp.max(s, axis=0, keepdims=True))
    alpha = jnp.exp2(m_old - m_next)
    p = jnp.exp2(s - m_next).astype(BF16)
    lanes = slice(base + lo, base + width)
    acc_ref[:, lanes] = alpha * acc_ref[:, lanes] + _dot(vt_blk, p)
    return m_next if lo == 0 else jnp.concatenate([m[:, :lo], m_next], axis=1)


def _stream(qt_ref, k_ref, vt_ref, acc_ref, s_ref, dims, base, width):
    tk = ATTN_K_BLOCK

    def scores(j, lo):
        start = pl.multiple_of(j * tk, tk)
        return _dot(k_ref[0, pl.ds(start, tk), dims], qt_ref[0, dims, base + lo:base + width])

    def consume(s, j, m, lo):
        return _online_softmax_step(s, vt_ref[0, j], m, acc_ref, base, lo, width)

    return scores, consume, s_ref, base, width


def _diagonal_mask(s):
    row = lax.broadcasted_iota(jnp.int32, s.shape, 0)
    col = lax.broadcasted_iota(jnp.int32, s.shape, 1)
    return jnp.where(row <= col, s, NEG_INF)


def _normalized(acc_ref):
    return acc_ref[:V_HEAD, :] / acc_ref[V_HEAD:V_HEAD + 1, :]


def _causal_key_loop(i, streams, tq):
    tk = ATTN_K_BLOCK
    group = tq // tk
    for scores, _, s_ref, _, _ in streams:
        s_ref[...] = scores(0, 0)

    def full_group(g, m):
        m = list(m)
        s_cur = [st[2][...] for st in streams]
        for d in range(group):
            j = g * group + d
            for n, (scores, consume, _, _, _) in enumerate(streams):
                s_next = scores(j + 1, 0)
                m[n] = consume(s_cur[n], j, m[n], 0)
                s_cur[n] = s_next
        for st, s in zip(streams, s_cur):
            st[2][...] = s
        return tuple(m)

    def diagonal_plan(d, base, width):
        first = d * tk - base
        if first >= width:
            return None
        return (first, True) if first >= 0 else (0, False)

    m0 = tuple(jnp.full((1, st[4]), NEG_INF, F32) for st in streams)
    m = list(lax.fori_loop(0, i, full_group, m0))
    s_cur = [st[2][...] for st in streams]
    for d in range(group):
        j = i * group + d
        for n, (scores, consume, _, base, width) in enumerate(streams):
            plan = diagonal_plan(d, base, width)
            ahead = diagonal_plan(d + 1, base, width) if d + 1 < group else None
            s_next = scores(j + 1, ahead[0]) if ahead else None
            if plan:
                s = _diagonal_mask(s_cur[n]) if plan[1] else s_cur[n]
                m[n] = consume(s, j, m[n], plan[0])
            s_cur[n] = s_next


def _mla_attn_kernel(qt_ref, k_ref, vt_ref, o_ref, acc_ref, *s_refs):
    acc_ref[...] = jnp.zeros(acc_ref.shape, F32)
    w = ATTN_STREAM_LANES
    streams = [_stream(qt_ref, k_ref, vt_ref, acc_ref, s_ref, slice(None), n * w, w)
               for n, s_ref in enumerate(s_refs)]
    _causal_key_loop(pl.program_id(1), streams, acc_ref.shape[1])
    o_ref[...] = _normalized(acc_ref).T.astype(o_ref.dtype)


def _mla_attn(qt, k, vt):
    tq, tk = MLA_Q_BLOCK, ATTN_K_BLOCK
    return pl.pallas_call(
        _mla_attn_kernel,
        grid=(MLA_HEADS, SEQ // tq),
        in_specs=[
            pl.BlockSpec((1, HEAD_PAD, tq), lambda h, i: (h, 0, i)),
            pl.BlockSpec((1, SEQ, HEAD_PAD), lambda h, i: (h, 0, 0)),
            pl.BlockSpec((1, SEQ // tk, V_ROWS, tk), lambda h, i: (h, 0, 0, 0)),
        ],
        out_specs=pl.BlockSpec((tq, V_HEAD), lambda h, i: (i, h)),
        out_shape=jax.ShapeDtypeStruct((SEQ, MLA_HEADS * V_HEAD), BF16),
        scratch_shapes=[pltpu.VMEM((V_ROWS, tq), F32)]
        + [pltpu.VMEM((tk, ATTN_STREAM_LANES), F32)] * (tq // ATTN_STREAM_LANES),
        compiler_params=pltpu.CompilerParams(
            dimension_semantics=("arbitrary", "arbitrary"), vmem_limit_bytes=VMEM_LIMIT),
        name="mla_attn",
    )(qt, k, vt)


def _ffn_kernel(h_ref, o_ref, w_o_ref, g_ffn_ref, w_gate_ref, w_up_ref, w_down_ref,
                g_final_ref, out_ref, *, final_norm):
    h1 = h_ref[...] + _dot(o_ref[...], w_o_ref[...])
    hn = ((h1 * _rms_scale(h1)) * g_ffn_ref[...]).astype(BF16)
    y = h1
    for c in range(D_FF // FF_CHUNK):
        g = _dot(hn, w_gate_ref[c])
        u = _dot(hn, w_up_ref[c])
        act = (g * (1.0 / (1.0 + jnp.exp(-g))) * u).astype(BF16)
        y = y + _dot(act, w_down_ref[c])
    if final_norm:
        y = (y * _rms_scale(y)) * g_final_ref[...]
    out_ref[...] = y


def _ffn(h, o, w_o, g_ffn, w_gate, w_up, w_down, g_final, *, final_norm):
    tm = ROW_BLOCK
    row = lambda i: (i, 0)
    return pl.pallas_call(
        functools.partial(_ffn_kernel, final_norm=final_norm),
        grid=(SEQ // tm,),
        in_specs=[
            pl.BlockSpec((tm, D_MODEL), row),
            pl.BlockSpec((tm, D_MODEL), row),
            _resident(w_o.shape),
            _full((1, D_MODEL)),
            _resident(w_gate.shape),
            _resident(w_up.shape),
            _resident(w_down.shape),
            _full((1, D_MODEL)),
        ],
        out_specs=pl.BlockSpec((tm, D_MODEL), row),
        out_shape=jax.ShapeDtypeStruct((SEQ, D_MODEL), F32),
        compiler_params=pltpu.CompilerParams(
            dimension_semantics=("arbitrary",), vmem_limit_bytes=VMEM_LIMIT),
        name="ffn_final" if final_norm else "ffn",
    )(h, o, w_o, g_ffn, w_gate, w_up, w_down, g_final)


def _pos_pieces(pos):
    a = pos >> POS_SHIFT_HI
    b = (pos >> POS_SHIFT_MID) & ((1 << (POS_SHIFT_HI - POS_SHIFT_MID)) - 1)
    c = pos & ((1 << POS_SHIFT_MID) - 1)
    return ((a << POS_SHIFT_HI).astype(F32), (b << POS_SHIFT_MID).astype(F32), c.astype(F32))


def _piece_pattern(idx, first, pieces):
    out = jnp.zeros(idx.shape, F32)
    for j, piece in enumerate(pieces):
        hit = (idx == first + j) | (idx == first + N_PIECES + j) | (idx == first + 2 * N_PIECES + j)
        out = jnp.where(hit, piece, out)
    return out


def _diff_proj_kernel(h_ref, pos_row_ref, pos_col_ref, g_kv_ref, g_attn_ref, w_qt_ref, w_k_ref,
                      w_vt_ref, q_const_ref, k_const_ref, qt_ref, k_ref, vt_ref):
    h = h_ref[...]
    tm = h.shape[0]
    hr = h * _rms_scale(h)
    hk = (hr * g_kv_ref[...]).astype(BF16)
    hq = (hr * g_attn_ref[...]).astype(BF16)
    qt = _dot_nt(w_qt_ref[...], hq)
    kf = _dot(hk, w_k_ref[...])
    vt = _dot_nt(w_vt_ref[...], hk)

    row = lax.broadcasted_iota(jnp.int32, (DIFF_HEAD, tm), 0)
    q_pos = _piece_pattern(row, N_BIAS, [-p for p in _pos_pieces(pos_row_ref[...])])
    lane = lax.broadcasted_iota(jnp.int32, (tm, LANE), 1)
    k_pos = _piece_pattern(lane, DIFF_HEAD, _pos_pieces(pos_col_ref[...]))

    q_scale = DIFF_HEAD ** -0.5 * LOG2E
    for hd in range(DIFF_HEADS):
        q_bias = (q_pos + q_const_ref[hd]).astype(BF16)
        k_bias = k_pos + k_const_ref[hd]
        k12 = kf[:, hd * DIFF_V:(hd + 1) * DIFF_V]
        k_halves = (k12, pltpu.roll(k12, DIFF_HEAD, 1))
        for half in range(2):
            lo = hd * DIFF_V + half * DIFF_HEAD
            qt_ref[hd, half * LANE:half * LANE + DIFF_HEAD, :] = (
                qt[lo:lo + DIFF_HEAD] * q_scale).astype(BF16)
            qt_ref[hd, half * LANE + DIFF_HEAD:(half + 1) * LANE, :] = q_bias
            k_ref[hd, :, half * LANE:(half + 1) * LANE] = jnp.where(
                lane < DIFF_HEAD, k_halves[half], k_bias).astype(BF16)
    _store_vt(vt_ref, vt, DIFF_HEADS)


def _diff_proj(h, pos_row, pos_col, g_kv, g_attn, w_qt, w_k, w_vt, q_const, k_const):
    tm, tk = ROW_BLOCK, ATTN_K_BLOCK
    return pl.pallas_call(
        _diff_proj_kernel,
        grid=(SEQ // tm,),
        in_specs=[
            pl.BlockSpec((tm, D_MODEL), lambda i: (i, 0)),
            pl.BlockSpec((1, tm), lambda i: (0, i)),
            pl.BlockSpec((tm, 1), lambda i: (i, 0)),
            _full((1, D_MODEL)),
            _full((1, D_MODEL)),
            _resident(w_qt.shape),
            _resident(w_k.shape),
            _resident(w_vt.shape),
            _full(q_const.shape),
            _full(k_const.shape),
        ],
        out_specs=[
            pl.BlockSpec((DIFF_HEADS, HEAD_PAD, tm), lambda i: (0, 0, i)),
            pl.BlockSpec((DIFF_HEADS, tm, HEAD_PAD), lambda i: (0, i, 0)),
            pl.BlockSpec((DIFF_HEADS, tm // tk, V_ROWS, tk), lambda i: (0, i, 0, 0)),
        ],
        out_shape=[
            jax.ShapeDtypeStruct((DIFF_HEADS, HEAD_PAD, SEQ), BF16),
            jax.ShapeDtypeStruct((DIFF_HEADS, SEQ, HEAD_PAD), BF16),
            jax.ShapeDtypeStruct((DIFF_HEADS, SEQ // tk, V_ROWS, tk), BF16),
        ],
        compiler_params=pltpu.CompilerParams(
            dimension_semantics=("arbitrary",), vmem_limit_bytes=VMEM_LIMIT),
        name="diff_proj",
    )(h, pos_row, pos_col, g_kv, g_attn, w_qt, w_k, w_vt, q_const, k_const)


def _diff_attn_kernel(lam_ref, qt_ref, k_ref, vt_ref, subln_ref, o_ref,
                      acc1_ref, acc2_ref, *s_refs):
    acc1_ref[...] = jnp.zeros(acc1_ref.shape, F32)
    acc2_ref[...] = jnp.zeros(acc2_ref.shape, F32)
    w = ATTN_STREAM_LANES
    per_half = len(s_refs) // 2
    streams = [
        _stream(qt_ref, k_ref, vt_ref, (acc1_ref, acc2_ref)[half], s_refs[n * 2 + half],
                slice(half * LANE, (half + 1) * LANE), n * w, w)
        for n in range(per_half) for half in range(2)]
    _causal_key_loop(pl.program_id(1), streams, acc1_ref.shape[1])

    lp = lam_ref[...]
    lam = (jnp.exp(jnp.sum(lp[0:1] * lp[1:2], axis=1, keepdims=True))
           - jnp.exp(jnp.sum(lp[2:3] * lp[3:4], axis=1, keepdims=True)) + LAMBDA_INIT)
    o = _normalized(acc1_ref) - lam * _normalized(acc2_ref)
    r = lax.rsqrt(jnp.mean(o * o, axis=0, keepdims=True) + EPS)
    o = (o * r).T * subln_ref[...] * (1.0 - LAMBDA_INIT)
    o_ref[...] = o.astype(o_ref.dtype)


def _diff_attn(lam, qt, k, vt, subln):
    tq, tk = DIFF_Q_BLOCK, ATTN_K_BLOCK
    acc = pltpu.VMEM((V_ROWS, tq), F32)
    s_tile = pltpu.VMEM((tk, ATTN_STREAM_LANES), F32)
    return pl.pallas_call(
        _diff_attn_kernel,
        grid=(DIFF_HEADS, SEQ // tq),
        in_specs=[
            pl.BlockSpec((4, DIFF_HEAD), lambda h, i: (0, 0)),
            pl.BlockSpec((1, HEAD_PAD, tq), lambda h, i: (h, 0, i)),
            pl.BlockSpec((1, SEQ, HEAD_PAD), lambda h, i: (h, 0, 0)),
            pl.BlockSpec((1, SEQ // tk, V_ROWS, tk), lambda h, i: (h, 0, 0, 0)),
            pl.BlockSpec((1, DIFF_V), lambda h, i: (0, 0)),
        ],
        out_specs=pl.BlockSpec((tq, DIFF_V), lambda h, i: (i, h)),
        out_shape=jax.ShapeDtypeStruct((SEQ, DIFF_HEADS * DIFF_V), BF16),
        scratch_shapes=[acc, acc] + [s_tile] * (2 * tq // ATTN_STREAM_LANES),
        compiler_params=pltpu.CompilerParams(
            dimension_semantics=("arbitrary", "arbitrary"), vmem_limit_bytes=VMEM_LIMIT),
        name="diff_attn",
    )(lam, qt, k, vt, subln)


def _split_ffn(w_gate_up, w_down):
    n = D_FF // FF_CHUNK
    w_gate = w_gate_up[:, :D_FF].reshape(D_MODEL, n, FF_CHUNK).transpose(1, 0, 2)
    w_up = w_gate_up[:, D_FF:].reshape(D_MODEL, n, FF_CHUNK).transpose(1, 0, 2)
    return w_gate.astype(BF16), w_up.astype(BF16), w_down.reshape(n, FF_CHUNK, D_MODEL).astype(BF16)


def _alibi_constants():
    slopes = 2.0 ** (-8.0 * jnp.arange(1, DIFF_HEADS + 1, dtype=F32) / DIFF_HEADS)
    c = slopes * LOG2E
    pieces = []
    for _ in range(N_PIECES):
        piece = c.astype(BF16).astype(F32)
        pieces.append(piece)
        c = c - piece
    per_dim = jnp.repeat(jnp.stack(pieces, axis=1), N_PIECES, axis=1)
    q_const = jnp.pad(per_dim, ((0, 0), (0, DIFF_HEAD - N_BIAS)))[:, :, None]
    k_const = jnp.pad(per_dim, ((0, 0), (DIFF_HEAD + N_BIAS, LANE - DIFF_HEAD - 2 * N_BIAS)))
    return q_const, k_const[:, None, :]


def kernel(x, positions, attn_norm, ffn_norm, final_norm, mla_w_dq, mla_q_norm, mla_w_uq,
           mla_w_dkv, mla_kv_norm, mla_w_ukv, mla_w_o, diff_kv_norm, diff_w_k, diff_w_v,
           diff_w_q, diff_lambda_q1, diff_lambda_k1, diff_lambda_q2, diff_lambda_k2,
           diff_subln, diff_w_o, ffn_w_gate_up, ffn_w_down):
    x2 = x.reshape(SEQ, D_MODEL)
    pos_row = positions.reshape(1, SEQ)
    pos_col = positions.reshape(SEQ, 1)
    row_vec = lambda a: a.reshape(1, -1)

    half = QK_ROPE // 2
    inv_col = (ROPE_THETA ** (-jnp.arange(half, dtype=F32) * 2.0 / QK_ROPE)).reshape(half, 1)

    w_dkv = mla_w_dkv[0]
    w_pet = jnp.pad(w_dkv[:, KV_LORA:].T, ((0, LANE - QK_ROPE), (0, 0))).astype(BF16)
    w_ukv = mla_w_ukv[0].reshape(KV_LORA, MLA_HEADS, QK_NOPE + V_HEAD)
    w_uk = w_ukv[:, :, :QK_NOPE].reshape(KV_LORA, MLA_HEADS * QK_NOPE).astype(BF16)
    w_uvt = w_ukv[:, :, QK_NOPE:].reshape(KV_LORA, MLA_HEADS * V_HEAD).T.astype(BF16)
    qt, k, vt = _mla_proj(
        x2, pos_row.astype(F32), inv_col, row_vec(attn_norm[0]), mla_w_dq[0].astype(BF16),
        row_vec(mla_q_norm[0]), mla_w_uq[0].T.astype(BF16), w_dkv[:, :KV_LORA].astype(BF16),
        w_pet, row_vec(mla_kv_norm[0]), w_uk, w_uvt)
    o = _mla_attn(qt, k, vt)
    w_gate, w_up, w_down = _split_ffn(ffn_w_gate_up[0], ffn_w_down[0])
    h = _ffn(x2, o, mla_w_o[0].astype(BF16), row_vec(ffn_norm[0]), w_gate, w_up, w_down,
             row_vec(final_norm), final_norm=False)

    q_const, k_const = _alibi_constants()
    qdt, kd, vdt = _diff_proj(h, pos_row, pos_col, row_vec(diff_kv_norm), row_vec(attn_norm[1]),
                              diff_w_q[0].T.astype(BF16), diff_w_k.astype(BF16),
                              diff_w_v.T.astype(BF16), q_const, k_const)
    lam = jnp.stack([diff_lambda_q1[0], diff_lambda_k1[0], diff_lambda_q2[0], diff_lambda_k2[0]])
    od = _diff_attn(lam, qdt, kd, vdt, row_vec(diff_subln[0]))
    w_gate, w_up, w_down = _split_ffn(ffn_w_gate_up[1], ffn_w_down[1])
    out = _ffn(h, od, diff_w_o[0].astype(BF16), row_vec(ffn_norm[1]), w_gate, w_up, w_down,
               row_vec(final_norm), final_norm=True)
    return out.reshape(x.shape)
```

```python
import functools
import math

import jax
import jax.numpy as jnp
import numpy as np
from jax import lax
from jax.experimental import pallas as pl
from jax.experimental.pallas import tpu as pltpu

D_MODEL = 1024
SEQ = 16384
MLA_HEADS = 8
QK_NOPE = 128
QK_ROPE = 64
V_HEAD = 128
Q_LORA = 512
KV_LORA = 256
ROPE_THETA = 10000.0
DIFF_HEADS = 8
DIFF_HEAD = 64
DIFF_V = 2 * DIFF_HEAD
D_FF = 2816
EPS = 1e-6
NEG_INF = -1e30
LAMBDA_INIT = 0.8 - 0.6 * float(np.exp(-0.3 * 1))

LANE = 128
HEAD_PAD = 2 * LANE
LOG2E = math.log2(math.e)
VMEM_LIMIT = 56 * 1024 * 1024

ROW_BLOCK = 512
MLA_Q_BLOCK = 2048
DIFF_Q_BLOCK = 2048
ATTN_K_BLOCK = 256
ATTN_STREAM_LANES = 256
MXU_TILE = 256
FF_SPLITS = (0, 6 * MXU_TILE, D_FF)
BF16_SUBLANES = 16
V_ROWS = V_HEAD + BF16_SUBLANES

POS_SHIFT_HI = 14
POS_SHIFT_MID = 7
N_PIECES = 3
N_BIAS = N_PIECES * N_PIECES

BF16 = jnp.bfloat16
F32 = jnp.float32


def _dot(a, b):
    return jnp.dot(a, b, preferred_element_type=F32)


def _dot_nt(a, b):
    return lax.dot_general(a, b, (((1,), (1,)), ((), ())), preferred_element_type=F32)


def _rms_scale(x):
    return lax.rsqrt(jnp.mean(x * x, axis=-1, keepdims=True) + EPS)


def _full(shape):
    return pl.BlockSpec(shape, lambda *_: (0,) * len(shape))


def _resident(shape):
    return pl.BlockSpec(shape, lambda *_: (0,) * len(shape), pipeline_mode=pl.Buffered(1))


def _ones_row_group(width):
    row = lax.broadcasted_iota(jnp.int32, (BF16_SUBLANES, width), 0)
    return jnp.where(row == 0, 1.0, 0.0).astype(BF16)


def _store_vt(vt_ref, vt, n_heads):
    tk = ATTN_K_BLOCK
    ones = _ones_row_group(tk)
    for h in range(n_heads):
        for c in range(vt.shape[1] // tk):
            vt_ref[h, c, :V_HEAD, :] = vt[h * V_HEAD:(h + 1) * V_HEAD, c * tk:(c + 1) * tk].astype(BF16)
            vt_ref[h, c, V_HEAD:, :] = ones


def _mla_proj_kernel(x_ref, pos_ref, inv_ref, g_attn_ref, w_dq_ref, g_q_ref, w_uqt_ref,
                     w_ckv_ref, w_pet_ref, g_kv_ref, w_uk_ref, w_uvt_ref,
                     qt_ref, k_ref, vt_ref):
    x = x_ref[...]
    hn = ((x * _rms_scale(x)) * g_attn_ref[...]).astype(BF16)

    cq = _dot(hn, w_dq_ref[...])
    cqn = ((cq * _rms_scale(cq)) * g_q_ref[...]).astype(BF16)
    qt = _dot_nt(w_uqt_ref[...], cqn)

    c = _dot(hn, w_ckv_ref[...])
    c_kv = ((c * _rms_scale(c)) * g_kv_ref[...]).astype(BF16)
    k_nope = _dot(c_kv, w_uk_ref[...])
    vt = _dot_nt(w_uvt_ref[...], c_kv)
    pet = _dot_nt(w_pet_ref[...], hn)

    ang = inv_ref[...] * pos_ref[...]
    cos = jnp.cos(ang)
    sin = jnp.sin(ang)
    half = QK_ROPE // 2

    def rope(x1, x2):
        return x1 * cos - x2 * sin, x2 * cos + x1 * sin

    r1, r2 = rope(pet[:half], pet[half:QK_ROPE])
    k_pe = jnp.concatenate([r1, r2, pet[QK_ROPE:]], axis=0).T.astype(BF16)

    q_scale = (QK_NOPE + QK_ROPE) ** -0.5 * LOG2E
    head_dim = QK_NOPE + QK_ROPE
    zero_rows = jnp.zeros((HEAD_PAD - head_dim, x.shape[0]), BF16)
    for h in range(MLA_HEADS):
        base = h * head_dim
        qt_ref[h, :QK_NOPE, :] = (qt[base:base + QK_NOPE] * q_scale).astype(BF16)
        r1, r2 = rope(qt[base + QK_NOPE:base + QK_NOPE + half],
                      qt[base + QK_NOPE + half:base + head_dim])
        qt_ref[h, QK_NOPE:QK_NOPE + half, :] = (r1 * q_scale).astype(BF16)
        qt_ref[h, QK_NOPE + half:head_dim, :] = (r2 * q_scale).astype(BF16)
        qt_ref[h, head_dim:, :] = zero_rows
        k_ref[h, :, :LANE] = k_nope[:, h * LANE:(h + 1) * LANE].astype(BF16)
        k_ref[h, :, LANE:] = k_pe
    _store_vt(vt_ref, vt, MLA_HEADS)


def _mla_proj(x, pos_row, inv_col, g_attn, w_dq, g_q, w_uqt, w_ckv, w_pet, g_kv, w_uk, w_uvt):
    tm, tk = ROW_BLOCK, ATTN_K_BLOCK
    return pl.pallas_call(
        _mla_proj_kernel,
        grid=(SEQ // tm,),
        in_specs=[
            pl.BlockSpec((tm, D_MODEL), lambda i: (i, 0)),
            pl.BlockSpec((1, tm), lambda i: (0, i)),
            _full(inv_col.shape),
            _full((1, D_MODEL)),
            _resident(w_dq.shape),
            _full((1, Q_LORA)),
            _resident(w_uqt.shape),
            _resident(w_ckv.shape),
            _resident(w_pet.shape),
            _full((1, KV_LORA)),
            _resident(w_uk.shape),
            _resident(w_uvt.shape),
        ],
        out_specs=[
            pl.BlockSpec((MLA_HEADS, HEAD_PAD, tm), lambda i: (0, 0, i)),
            pl.BlockSpec((MLA_HEADS, tm, HEAD_PAD), lambda i: (0, i, 0)),
            pl.BlockSpec((MLA_HEADS, tm // tk, V_ROWS, tk), lambda i: (0, i, 0, 0)),
        ],
        out_shape=[
            jax.ShapeDtypeStruct((MLA_HEADS, HEAD_PAD, SEQ), BF16),
            jax.ShapeDtypeStruct((MLA_HEADS, SEQ, HEAD_PAD), BF16),
            jax.ShapeDtypeStruct((MLA_HEADS, SEQ // tk, V_ROWS, tk), BF16),
        ],
        compiler_params=pltpu.CompilerParams(
            dimension_semantics=("arbitrary",), vmem_limit_bytes=VMEM_LIMIT),
        name="mla_proj",
    )(x, pos_row, inv_col, g_attn, w_dq, g_q, w_uqt, w_ckv, w_pet, g_kv, w_uk, w_uvt)


def _online_softmax_step(s, vt_blk, m, acc_ref, base, lo, width):
    m_old = m[:, lo:]
    m_next = jnp.maximum(m_old, jnp.max(s, axis=0, keepdims=True))
    alpha = jnp.exp2(m_old - m_next)
    p = jnp.exp2(s - m_next).astype(BF16)
    lanes = slice(base + lo, base + width)
    acc_ref[:, lanes] = alpha * acc_ref[:, lanes] + _dot(vt_blk, p)
    return m_next if lo == 0 else jnp.concatenate([m[:, :lo], m_next], axis=1)


def _stream(qt_ref, k_ref, vt_ref, acc_ref, s_ref, dims, base, width):
    tk = ATTN_K_BLOCK

    def scores(j, lo):
        start = pl.multiple_of(j * tk, tk)
        return _dot(k_ref[0, pl.ds(start, tk), dims], qt_ref[0, dims, base + lo:base + width])

    def consume(s, j, m, lo):
        return _online_softmax_step(s, vt_ref[0, j], m, acc_ref, base, lo, width)

    return scores, consume, s_ref, base, width


def _diagonal_mask(s):
    row = lax.broadcasted_iota(jnp.int32, s.shape, 0)
    col = lax.broadcasted_iota(jnp.int32, s.shape, 1)
    return jnp.where(row <= col, s, NEG_INF)


def _normalized(acc_ref):
    return acc_ref[:V_HEAD, :] / acc_ref[V_HEAD:V_HEAD + 1, :]


def _causal_key_loop(i, streams, tq):
    tk = ATTN_K_BLOCK
    group = tq // tk
    for scores, _, s_ref, _, _ in streams:
        s_ref[...] = scores(0, 0)

    def full_group(g, m):
        m = list(m)
        s_cur = [st[2][...] for st in streams]
        for d in range(group):
            j = g * group + d
            for n, (scores, consume, _, _, _) in enumerate(streams):
                s_next = scores(j + 1, 0)
                m[n] = consume(s_cur[n], j, m[n], 0)
                s_cur[n] = s_next
        for st, s in zip(streams, s_cur):
            st[2][...] = s
        return tuple(m)

    def diagonal_plan(d, base, width):
        first = d * tk - base
        if first >= width:
            return None
        return (first, True) if first >= 0 else (0, False)

    m0 = tuple(jnp.full((1, st[4]), NEG_INF, F32) for st in streams)
    m = list(lax.fori_loop(0, i, full_group, m0))
    s_cur = [st[2][...] for st in streams]
    for d in range(group):
        j = i * group + d
        for n, (scores, consume, _, base, width) in enumerate(streams):
            plan = diagonal_plan(d, base, width)
            ahead = diagonal_plan(d + 1, base, width) if d + 1 < group else None
            s_next = scores(j + 1, ahead[0]) if ahead else None
            if plan:
                s = _diagonal_mask(s_cur[n]) if plan[1] else s_cur[n]
                m[n] = consume(s, j, m[n], plan[0])
            s_cur[n] = s_next


def _mla_attn_kernel(qt_ref, k_ref, vt_ref, o_ref, acc_ref, *s_refs):
    acc_ref[...] = jnp.zeros(acc_ref.shape, F32)
    w = ATTN_STREAM_LANES
    streams = [_stream(qt_ref, k_ref, vt_ref, acc_ref, s_ref, slice(None), n * w, w)
               for n, s_ref in enumerate(s_refs)]
    _causal_key_loop(pl.program_id(1), streams, acc_ref.shape[1])
    o_ref[...] = _normalized(acc_ref).T.astype(o_ref.dtype)


def _mla_attn(qt, k, vt):
    tq, tk = MLA_Q_BLOCK, ATTN_K_BLOCK
    return pl.pallas_call(
        _mla_attn_kernel,
        grid=(MLA_HEADS, SEQ // tq),
        in_specs=[
            pl.BlockSpec((1, HEAD_PAD, tq), lambda h, i: (h, 0, i)),
            pl.BlockSpec((1, SEQ, HEAD_PAD), lambda h, i: (h, 0, 0)),
            pl.BlockSpec((1, SEQ // tk, V_ROWS, tk), lambda h, i: (h, 0, 0, 0)),
        ],
        out_specs=pl.BlockSpec((tq, V_HEAD), lambda h, i: (i, h)),
        out_shape=jax.ShapeDtypeStruct((SEQ, MLA_HEADS * V_HEAD), BF16),
        scratch_shapes=[pltpu.VMEM((V_ROWS, tq), F32)]
        + [pltpu.VMEM((tk, ATTN_STREAM_LANES), F32)] * (tq // ATTN_STREAM_LANES),
        compiler_params=pltpu.CompilerParams(
            dimension_semantics=("arbitrary", "arbitrary"), vmem_limit_bytes=VMEM_LIMIT),
        name="mla_attn",
    )(qt, k, vt)


def _ffn_kernel(h_ref, o_ref, w_o_ref, g_ffn_ref, w_gate_up_ref, w_down_ref,
                g_final_ref, out_ref, *, final_norm):
    h1 = h_ref[...] + _dot(o_ref[...], w_o_ref[...])
    hn = ((h1 * _rms_scale(h1)) * g_ffn_ref[...]).astype(BF16)
    y = h1
    for lo, hi in zip(FF_SPLITS[:-1], FF_SPLITS[1:]):
        g = _dot(hn, w_gate_up_ref[:, lo:hi])
        u = _dot(hn, w_gate_up_ref[:, D_FF + lo:D_FF + hi])
        act = (g * (1.0 / (1.0 + jnp.exp(-g))) * u).astype(BF16)
        y = y + _dot(act, w_down_ref[lo:hi, :])
    if final_norm:
        y = (y * _rms_scale(y)) * g_final_ref[...]
    out_ref[...] = y


def _ffn(h, o, w_o, g_ffn, w_gate_up, w_down, g_final, *, final_norm):
    tm = ROW_BLOCK
    row = lambda i: (i, 0)
    return pl.pallas_call(
        functools.partial(_ffn_kernel, final_norm=final_norm),
        grid=(SEQ // tm,),
        in_specs=[
            pl.BlockSpec((tm, D_MODEL), row),
            pl.BlockSpec((tm, D_MODEL), row),
            _resident(w_o.shape),
            _full((1, D_MODEL)),
            _resident(w_gate_up.shape),
            _resident(w_down.shape),
            _full((1, D_MODEL)),
        ],
        out_specs=pl.BlockSpec((tm, D_MODEL), row),
        out_shape=jax.ShapeDtypeStruct((SEQ, D_MODEL), F32),
        compiler_params=pltpu.CompilerParams(
            dimension_semantics=("arbitrary",), vmem_limit_bytes=VMEM_LIMIT),
        name="ffn_final" if final_norm else "ffn",
    )(h, o, w_o, g_ffn, w_gate_up, w_down, g_final)


def _pos_pieces(pos):
    a = pos >> POS_SHIFT_HI
    b = (pos >> POS_SHIFT_MID) & ((1 << (POS_SHIFT_HI - POS_SHIFT_MID)) - 1)
    c = pos & ((1 << POS_SHIFT_MID) - 1)
    return ((a << POS_SHIFT_HI).astype(F32), (b << POS_SHIFT_MID).astype(F32), c.astype(F32))


def _piece_pattern(idx, first, pieces):
    out = jnp.zeros(idx.shape, F32)
    for j, piece in enumerate(pieces):
        hit = (idx == first + j) | (idx == first + N_PIECES + j) | (idx == first + 2 * N_PIECES + j)
        out = jnp.where(hit, piece, out)
    return out


def _diff_proj_kernel(h_ref, pos_row_ref, pos_col_ref, g_kv_ref, g_attn_ref, w_qt_ref, w_k_ref,
                      w_vt_ref, q_const_ref, k_const_ref, qt_ref, k_ref, vt_ref):
    h = h_ref[...]
    tm = h.shape[0]
    hr = h * _rms_scale(h)
    hk = (hr * g_kv_ref[...]).astype(BF16)
    hq = (hr * g_attn_ref[...]).astype(BF16)
    qt = _dot_nt(w_qt_ref[...], hq)
    kf = _dot(hk, w_k_ref[...])
    vt = _dot_nt(w_vt_ref[...], hk)

    row = lax.broadcasted_iota(jnp.int32, (DIFF_HEAD, tm), 0)
    q_pos = _piece_pattern(row, N_BIAS, [-p for p in _pos_pieces(pos_row_ref[...])])
    lane = lax.broadcasted_iota(jnp.int32, (tm, LANE), 1)
    k_pos = _piece_pattern(lane, DIFF_HEAD, _pos_pieces(pos_col_ref[...]))

    q_scale = DIFF_HEAD ** -0.5 * LOG2E
    for hd in range(DIFF_HEADS):
        q_bias = (q_pos + q_const_ref[hd]).astype(BF16)
        k_bias = k_pos + k_const_ref[hd]
        k12 = kf[:, hd * DIFF_V:(hd + 1) * DIFF_V]
        k_halves = (k12, pltpu.roll(k12, DIFF_HEAD, 1))
        for half in range(2):
            lo = hd * DIFF_V + half * DIFF_HEAD
            qt_ref[hd, half * LANE:half * LANE + DIFF_HEAD, :] = (
                qt[lo:lo + DIFF_HEAD] * q_scale).astype(BF16)
            qt_ref[hd, half * LANE + DIFF_HEAD:(half + 1) * LANE, :] = q_bias
            k_ref[hd, :, half * LANE:(half + 1) * LANE] = jnp.where(
                lane < DIFF_HEAD, k_halves[half], k_bias).astype(BF16)
    _store_vt(vt_ref, vt, DIFF_HEADS)


def _diff_proj(h, pos_row, pos_col, g_kv, g_attn, w_qt, w_k, w_vt, q_const, k_const):
    tm, tk = ROW_BLOCK, ATTN_K_BLOCK
    return pl.pallas_call(
        _diff_proj_kernel,
        grid=(SEQ // tm,),
        in_specs=[
            pl.BlockSpec((tm, D_MODEL), lambda i: (i, 0)),
            pl.BlockSpec((1, tm), lambda i: (0, i)),
            pl.BlockSpec((tm, 1), lambda i: (i, 0)),
            _full((1, D_MODEL)),
            _full((1, D_MODEL)),
            _resident(w_qt.shape),
            _resident(w_k.shape),
            _resident(w_vt.shape),
            _full(q_const.shape),
            _full(k_const.shape),
        ],
        out_specs=[
            pl.BlockSpec((DIFF_HEADS, HEAD_PAD, tm), lambda i: (0, 0, i)),
            pl.BlockSpec((DIFF_HEADS, tm, HEAD_PAD), lambda i: (0, i, 0)),
            pl.BlockSpec((DIFF_HEADS, tm // tk, V_ROWS, tk), lambda i: (0, i, 0, 0)),
        ],
        out_shape=[
            jax.ShapeDtypeStruct((DIFF_HEADS, HEAD_PAD, SEQ), BF16),
            jax.ShapeDtypeStruct((DIFF_HEADS, SEQ, HEAD_PAD), BF16),
            jax.ShapeDtypeStruct((DIFF_HEADS, SEQ // tk, V_ROWS, tk), BF16),
        ],
        compiler_params=pltpu.CompilerParams(
            dimension_semantics=("arbitrary",), vmem_limit_bytes=VMEM_LIMIT),
        name="diff_proj",
    )(h, pos_row, pos_col, g_kv, g_attn, w_qt, w_k, w_vt, q_const, k_const)


def _diff_attn_kernel(lam_ref, qt_ref, k_ref, vt_ref, subln_ref, o_ref,
                      acc1_ref, acc2_ref, *s_refs):
    acc1_ref[...] = jnp.zeros(acc1_ref.shape, F32)
    acc2_ref[...] = jnp.zeros(acc2_ref.shape, F32)
    w = ATTN_STREAM_LANES
    per_half = len(s_refs) // 2
    streams = [
        _stream(qt_ref, k_ref, vt_ref, (acc1_ref, acc2_ref)[half], s_refs[n * 2 + half],
                slice(half * LANE, (half + 1) * LANE), n * w, w)
        for n in range(per_half) for half in range(2)]
    _causal_key_loop(pl.program_id(1), streams, acc1_ref.shape[1])

    lp = lam_ref[...]
    lam = (jnp.exp(jnp.sum(lp[0:1] * lp[1:2], axis=1, keepdims=True))
           - jnp.exp(jnp.sum(lp[2:3] * lp[3:4], axis=1, keepdims=True)) + LAMBDA_INIT)
    o = _normalized(acc1_ref) - lam * _normalized(acc2_ref)
    r = lax.rsqrt(jnp.mean(o * o, axis=0, keepdims=True) + EPS)
    o = (o * r).T * subln_ref[...] * (1.0 - LAMBDA_INIT)
    o_ref[...] = o.astype(o_ref.dtype)


def _diff_attn(lam, qt, k, vt, subln):
    tq, tk = DIFF_Q_BLOCK, ATTN_K_BLOCK
    acc = pltpu.VMEM((V_ROWS, tq), F32)
    s_tile = pltpu.VMEM((tk, ATTN_STREAM_LANES), F32)
    return pl.pallas_call(
        _diff_attn_kernel,
        grid=(DIFF_HEADS, SEQ // tq),
        in_specs=[
            pl.BlockSpec((4, DIFF_HEAD), lambda h, i: (0, 0)),
            pl.BlockSpec((1, HEAD_PAD, tq), lambda h, i: (h, 0, i)),
            pl.BlockSpec((1, SEQ, HEAD_PAD), lambda h, i: (h, 0, 0)),
            pl.BlockSpec((1, SEQ // tk, V_ROWS, tk), lambda h, i: (h, 0, 0, 0)),
            pl.BlockSpec((1, DIFF_V), lambda h, i: (0, 0)),
        ],
        out_specs=pl.BlockSpec((tq, DIFF_V), lambda h, i: (i, h)),
        out_shape=jax.ShapeDtypeStruct((SEQ, DIFF_HEADS * DIFF_V), BF16),
        scratch_shapes=[acc, acc] + [s_tile] * (2 * tq // ATTN_STREAM_LANES),
        compiler_params=pltpu.CompilerParams(
            dimension_semantics=("arbitrary", "arbitrary"), vmem_limit_bytes=VMEM_LIMIT),
        name="diff_attn",
    )(lam, qt, k, vt, subln)


def _alibi_constants():
    slopes = 2.0 ** (-8.0 * jnp.arange(1, DIFF_HEADS + 1, dtype=F32) / DIFF_HEADS)
    c = slopes * LOG2E
    pieces = []
    for _ in range(N_PIECES):
        piece = c.astype(BF16).astype(F32)
        pieces.append(piece)
        c = c - piece
    per_dim = jnp.repeat(jnp.stack(pieces, axis=1), N_PIECES, axis=1)
    q_const = jnp.pad(per_dim, ((0, 0), (0, DIFF_HEAD - N_BIAS)))[:, :, None]
    k_const = jnp.pad(per_dim, ((0, 0), (DIFF_HEAD + N_BIAS, LANE - DIFF_HEAD - 2 * N_BIAS)))
    return q_const, k_const[:, None, :]


def kernel(x, positions, attn_norm, ffn_norm, final_norm, mla_w_dq, mla_q_norm, mla_w_uq,
           mla_w_dkv, mla_kv_norm, mla_w_ukv, mla_w_o, diff_kv_norm, diff_w_k, diff_w_v,
           diff_w_q, diff_lambda_q1, diff_lambda_k1, diff_lambda_q2, diff_lambda_k2,
           diff_subln, diff_w_o, ffn_w_gate_up, ffn_w_down):
    x2 = x.reshape(SEQ, D_MODEL)
    pos_row = positions.reshape(1, SEQ)
    pos_col = positions.reshape(SEQ, 1)
    row_vec = lambda a: a.reshape(1, -1)

    half = QK_ROPE // 2
    inv_col = (ROPE_THETA ** (-jnp.arange(half, dtype=F32) * 2.0 / QK_ROPE)).reshape(half, 1)

    w_dkv = mla_w_dkv[0]
    w_pet = jnp.pad(w_dkv[:, KV_LORA:].T, ((0, LANE - QK_ROPE), (0, 0))).astype(BF16)
    w_ukv = mla_w_ukv[0].reshape(KV_LORA, MLA_HEADS, QK_NOPE + V_HEAD)
    w_uk = w_ukv[:, :, :QK_NOPE].reshape(KV_LORA, MLA_HEADS * QK_NOPE).astype(BF16)
    w_uvt = w_ukv[:, :, QK_NOPE:].reshape(KV_LORA, MLA_HEADS * V_HEAD).T.astype(BF16)
    qt, k, vt = _mla_proj(
        x2, pos_row.astype(F32), inv_col, row_vec(attn_norm[0]), mla_w_dq[0].astype(BF16),
        row_vec(mla_q_norm[0]), mla_w_uq[0].T.astype(BF16), w_dkv[:, :KV_LORA].astype(BF16),
        w_pet, row_vec(mla_kv_norm[0]), w_uk, w_uvt)
    o = _mla_attn(qt, k, vt)
    h = _ffn(x2, o, mla_w_o[0].astype(BF16), row_vec(ffn_norm[0]), ffn_w_gate_up[0].astype(BF16),
             ffn_w_down[0].astype(BF16), row_vec(final_norm), final_norm=False)

    q_const, k_const = _alibi_constants()
    qdt, kd, vdt = _diff_proj(h, pos_row, pos_col, row_vec(diff_kv_norm), row_vec(attn_norm[1]),
                              diff_w_q[0].T.astype(BF16), diff_w_k.astype(BF16),
                              diff_w_v.T.astype(BF16), q_const, k_const)
    lam = jnp.stack([diff_lambda_q1[0], diff_lambda_k1[0], diff_lambda_q2[0], diff_lambda_k2[0]])
    od = _diff_attn(lam, qdt, kd, vdt, row_vec(diff_subln[0]))
    out = _ffn(h, od, diff_w_o[0].astype(BF16), row_vec(ffn_norm[1]),
               ffn_w_gate_up[1].astype(BF16), ffn_w_down[1].astype(BF16), row_vec(final_norm),
               final_norm=True)
    return out.reshape(x.shape)
```

```python
import functools
import math

import jax
import jax.numpy as jnp
import numpy as np
from jax import lax
from jax.experimental import pallas as pl
from jax.experimental.pallas import tpu as pltpu

D_MODEL = 1024
SEQ = 16384
MLA_HEADS = 8
QK_NOPE = 128
QK_ROPE = 64
V_HEAD = 128
Q_LORA = 512
KV_LORA = 256
ROPE_THETA = 10000.0
DIFF_HEADS = 8
DIFF_HEAD = 64
DIFF_V = 2 * DIFF_HEAD
D_FF = 2816
EPS = 1e-6
NEG_INF = -1e30
LAMBDA_INIT = 0.8 - 0.6 * float(np.exp(-0.3 * 1))

LANE = 128
HEAD_PAD = 2 * LANE
LOG2E = math.log2(math.e)
VMEM_LIMIT = 56 * 1024 * 1024

ROW_BLOCK = 512
PROJ_ROW_BLOCK = 1024
MLA_Q_BLOCK = 2048
DIFF_Q_BLOCK = 2048
ATTN_K_BLOCK = 256
ATTN_STREAM_LANES = 256
MXU_TILE = 256
FF_SPLITS = (0, 6 * MXU_TILE, D_FF)
BF16_SUBLANES = 16
V_ROWS = V_HEAD + BF16_SUBLANES

POS_SHIFT_HI = 14
POS_SHIFT_MID = 7
N_PIECES = 3
N_BIAS = N_PIECES * N_PIECES

BF16 = jnp.bfloat16
F32 = jnp.float32


def _dot(a, b):
    return jnp.dot(a, b, preferred_element_type=F32)


def _dot_nt(a, b):
    return lax.dot_general(a, b, (((1,), (1,)), ((), ())), preferred_element_type=F32)


def _rms_scale(x):
    return lax.rsqrt(jnp.mean(x * x, axis=-1, keepdims=True) + EPS)


def _full(shape):
    return pl.BlockSpec(shape, lambda *_: (0,) * len(shape))


def _resident(shape):
    return pl.BlockSpec(shape, lambda *_: (0,) * len(shape), pipeline_mode=pl.Buffered(1))


def _ones_row_group(width):
    row = lax.broadcasted_iota(jnp.int32, (BF16_SUBLANES, width), 0)
    return jnp.where(row == 0, 1.0, 0.0).astype(BF16)


def _store_vt(vt_ref, vt, n_heads):
    tk = ATTN_K_BLOCK
    ones = _ones_row_group(tk)
    for h in range(n_heads):
        for c in range(vt.shape[1] // tk):
            vt_ref[h, c, :V_HEAD, :] = vt[h * V_HEAD:(h + 1) * V_HEAD, c * tk:(c + 1) * tk].astype(BF16)
            vt_ref[h, c, V_HEAD:, :] = ones


def _mla_proj_kernel(x_ref, pos_ref, inv_ref, g_attn_ref, w_dq_ref, g_q_ref, w_uqt_ref,
                     w_ckv_ref, w_pet_ref, g_kv_ref, w_uk_ref, w_uvt_ref,
                     qt_ref, k_ref, vt_ref):
    x = x_ref[...]
    hn = ((x * _rms_scale(x)) * g_attn_ref[...]).astype(BF16)

    cq = _dot(hn, w_dq_ref[...])
    cqn = ((cq * _rms_scale(cq)) * g_q_ref[...]).astype(BF16)
    qt = _dot_nt(w_uqt_ref[...], cqn)

    c = _dot(hn, w_ckv_ref[...])
    c_kv = ((c * _rms_scale(c)) * g_kv_ref[...]).astype(BF16)
    k_nope = _dot(c_kv, w_uk_ref[...])
    vt = _dot_nt(w_uvt_ref[...], c_kv)
    pet = _dot_nt(w_pet_ref[...], hn)

    ang = inv_ref[...] * pos_ref[...]
    cos = jnp.cos(ang)
    sin = jnp.sin(ang)
    half = QK_ROPE // 2

    def rope(x1, x2):
        return x1 * cos - x2 * sin, x2 * cos + x1 * sin

    r1, r2 = rope(pet[:half], pet[half:QK_ROPE])
    k_pe = jnp.concatenate([r1, r2, pet[QK_ROPE:]], axis=0).T.astype(BF16)

    q_scale = (QK_NOPE + QK_ROPE) ** -0.5 * LOG2E
    head_dim = QK_NOPE + QK_ROPE
    zero_rows = jnp.zeros((HEAD_PAD - head_dim, x.shape[0]), BF16)
    for h in range(MLA_HEADS):
        base = h * head_dim
        qt_ref[h, :QK_NOPE, :] = (qt[base:base + QK_NOPE] * q_scale).astype(BF16)
        r1, r2 = rope(qt[base + QK_NOPE:base + QK_NOPE + half],
                      qt[base + QK_NOPE + half:base + head_dim])
        qt_ref[h, QK_NOPE:QK_NOPE + half, :] = (r1 * q_scale).astype(BF16)
        qt_ref[h, QK_NOPE + half:head_dim, :] = (r2 * q_scale).astype(BF16)
        qt_ref[h, head_dim:, :] = zero_rows
        k_ref[h, :, :LANE] = k_nope[:, h * LANE:(h + 1) * LANE].astype(BF16)
        k_ref[h, :, LANE:] = k_pe
    _store_vt(vt_ref, vt, MLA_HEADS)


def _mla_proj(x, pos_row, inv_col, g_attn, w_dq, g_q, w_uqt, w_ckv, w_pet, g_kv, w_uk, w_uvt):
    tm, tk = PROJ_ROW_BLOCK, ATTN_K_BLOCK
    return pl.pallas_call(
        _mla_proj_kernel,
        grid=(SEQ // tm,),
        in_specs=[
            pl.BlockSpec((tm, D_MODEL), lambda i: (i, 0)),
            pl.BlockSpec((1, tm), lambda i: (0, i)),
            _full(inv_col.shape),
            _full((1, D_MODEL)),
            _resident(w_dq.shape),
            _full((1, Q_LORA)),
            _resident(w_uqt.shape),
            _resident(w_ckv.shape),
            _resident(w_pet.shape),
            _full((1, KV_LORA)),
            _resident(w_uk.shape),
            _resident(w_uvt.shape),
        ],
        out_specs=[
            pl.BlockSpec((MLA_HEADS, HEAD_PAD, tm), lambda i: (0, 0, i)),
            pl.BlockSpec((MLA_HEADS, tm, HEAD_PAD), lambda i: (0, i, 0)),
            pl.BlockSpec((MLA_HEADS, tm // tk, V_ROWS, tk), lambda i: (0, i, 0, 0)),
        ],
        out_shape=[
            jax.ShapeDtypeStruct((MLA_HEADS, HEAD_PAD, SEQ), BF16),
            jax.ShapeDtypeStruct((MLA_HEADS, SEQ, HEAD_PAD), BF16),
            jax.ShapeDtypeStruct((MLA_HEADS, SEQ // tk, V_ROWS, tk), BF16),
        ],
        compiler_params=pltpu.CompilerParams(
            dimension_semantics=("arbitrary",), vmem_limit_bytes=VMEM_LIMIT),
        name="mla_proj",
    )(x, pos_row, inv_col, g_attn, w_dq, g_q, w_uqt, w_ckv, w_pet, g_kv, w_uk, w_uvt)


def _online_softmax_step(s, vt_blk, m, acc_ref, base, lo, width):
    m_old = m[:, lo:]
    m_next = jnp.maximum(m_old, jnp.max(s, axis=0, keepdims=True))
    alpha = jnp.exp2(m_old - m_next)
    p = jnp.exp2(s - m_next).astype(BF16)
    lanes = slice(base + lo, base + width)
    acc_ref[:, lanes] = alpha * acc_ref[:, lanes] + _dot(vt_blk, p)
    return m_next if lo == 0 else jnp.concatenate([m[:, :lo], m_next], axis=1)


def _stream(qt_ref, k_ref, vt_ref, acc_ref, s_ref, dims, base, width):
    tk = ATTN_K_BLOCK

    def scores(j, lo):
        start = pl.multiple_of(j * tk, tk)
        return _dot(k_ref[0, pl.ds(start, tk), dims], qt_ref[0, dims, base + lo:base + width])

    def consume(s, j, m, lo):
        return _online_softmax_step(s, vt_ref[0, j], m, acc_ref, base, lo, width)

    return scores, consume, s_ref, base, width


def _diagonal_mask(s):
    row = lax.broadcasted_iota(jnp.int32, s.shape, 0)
    col = lax.broadcasted_iota(jnp.int32, s.shape, 1)
    return jnp.where(row <= col, s, NEG_INF)


def _normalized(acc_ref):
    return acc_ref[:V_HEAD, :] / acc_ref[V_HEAD:V_HEAD + 1, :]


def _causal_key_loop(i, streams, tq):
    tk = ATTN_K_BLOCK
    group = tq // tk
    for scores, _, s_ref, _, _ in streams:
        s_ref[...] = scores(0, 0)

    def full_group(g, m):
        m = list(m)
        s_cur = [st[2][...] for st in streams]
        for d in range(group):
            j = g * group + d
            for n, (scores, consume, _, _, _) in enumerate(streams):
                s_next = scores(j + 1, 0)
                m[n] = consume(s_cur[n], j, m[n], 0)
                s_cur[n] = s_next
        for st, s in zip(streams, s_cur):
            st[2][...] = s
        return tuple(m)

    def diagonal_plan(d, base, width):
        first = d * tk - base
        if first >= width:
            return None
        return (first, True) if first >= 0 else (0, False)

    m0 = tuple(jnp.full((1, st[4]), NEG_INF, F32) for st in streams)
    m = list(lax.fori_loop(0, i, full_group, m0))
    s_cur = [st[2][...] for st in streams]
    for d in range(group):
        j = i * group + d
        for n, (scores, consume, _, base, width) in enumerate(streams):
            plan = diagonal_plan(d, base, width)
            ahead = diagonal_plan(d + 1, base, width) if d + 1 < group else None
            s_next = scores(j + 1, ahead[0]) if ahead else None
            if plan:
                s = _diagonal_mask(s_cur[n]) if plan[1] else s_cur[n]
                m[n] = consume(s, j, m[n], plan[0])
            s_cur[n] = s_next


def _mla_attn_kernel(qt_ref, k_ref, vt_ref, o_ref, acc_ref, *s_refs):
    acc_ref[...] = jnp.zeros(acc_ref.shape, F32)
    w = ATTN_STREAM_LANES
    streams = [_stream(qt_ref, k_ref, vt_ref, acc_ref, s_ref, slice(None), n * w, w)
               for n, s_ref in enumerate(s_refs)]
    _causal_key_loop(pl.program_id(1), streams, acc_ref.shape[1])
    o_ref[...] = _normalized(acc_ref).T.astype(o_ref.dtype)


def _mla_attn(qt, k, vt):
    tq, tk = MLA_Q_BLOCK, ATTN_K_BLOCK
    return pl.pallas_call(
        _mla_attn_kernel,
        grid=(MLA_HEADS, SEQ // tq),
        in_specs=[
            pl.BlockSpec((1, HEAD_PAD, tq), lambda h, i: (h, 0, i)),
            pl.BlockSpec((1, SEQ, HEAD_PAD), lambda h, i: (h, 0, 0)),
            pl.BlockSpec((1, SEQ // tk, V_ROWS, tk), lambda h, i: (h, 0, 0, 0)),
        ],
        out_specs=pl.BlockSpec((tq, V_HEAD), lambda h, i: (i, h)),
        out_shape=jax.ShapeDtypeStruct((SEQ, MLA_HEADS * V_HEAD), BF16),
        scratch_shapes=[pltpu.VMEM((V_ROWS, tq), F32)]
        + [pltpu.VMEM((tk, ATTN_STREAM_LANES), F32)] * (tq // ATTN_STREAM_LANES),
        compiler_params=pltpu.CompilerParams(
            dimension_semantics=("arbitrary", "arbitrary"), vmem_limit_bytes=VMEM_LIMIT),
        name="mla_attn",
    )(qt, k, vt)


def _ffn_kernel(h_ref, o_ref, w_o_ref, g_ffn_ref, w_gate_up_ref, w_down_ref,
                g_final_ref, out_ref, *, final_norm):
    h1 = h_ref[...] + _dot(o_ref[...], w_o_ref[...])
    hn = ((h1 * _rms_scale(h1)) * g_ffn_ref[...]).astype(BF16)
    y = h1
    for lo, hi in zip(FF_SPLITS[:-1], FF_SPLITS[1:]):
        g = _dot(hn, w_gate_up_ref[:, lo:hi])
        u = _dot(hn, w_gate_up_ref[:, D_FF + lo:D_FF + hi])
        act = (g * (1.0 / (1.0 + jnp.exp(-g))) * u).astype(BF16)
        y = y + _dot(act, w_down_ref[lo:hi, :])
    if final_norm:
        y = (y * _rms_scale(y)) * g_final_ref[...]
    out_ref[...] = y


def _ffn(h, o, w_o, g_ffn, w_gate_up, w_down, g_final, *, final_norm):
    tm = ROW_BLOCK
    row = lambda i: (i, 0)
    return pl.pallas_call(
        functools.partial(_ffn_kernel, final_norm=final_norm),
        grid=(SEQ // tm,),
        in_specs=[
            pl.BlockSpec((tm, D_MODEL), row),
            pl.BlockSpec((tm, D_MODEL), row),
            _resident(w_o.shape),
            _full((1, D_MODEL)),
            _resident(w_gate_up.shape),
            _resident(w_down.shape),
            _full((1, D_MODEL)),
        ],
        out_specs=pl.BlockSpec((tm, D_MODEL), row),
        out_shape=jax.ShapeDtypeStruct((SEQ, D_MODEL), F32),
        compiler_params=pltpu.CompilerParams(
            dimension_semantics=("arbitrary",), vmem_limit_bytes=VMEM_LIMIT),
        name="ffn_final" if final_norm else "ffn",
    )(h, o, w_o, g_ffn, w_gate_up, w_down, g_final)


def _pos_pieces(pos):
    a = pos >> POS_SHIFT_HI
    b = (pos >> POS_SHIFT_MID) & ((1 << (POS_SHIFT_HI - POS_SHIFT_MID)) - 1)
    c = pos & ((1 << POS_SHIFT_MID) - 1)
    return ((a << POS_SHIFT_HI).astype(F32), (b << POS_SHIFT_MID).astype(F32), c.astype(F32))


def _piece_pattern(idx, first, pieces):
    out = jnp.zeros(idx.shape, F32)
    for j, piece in enumerate(pieces):
        hit = (idx == first + j) | (idx == first + N_PIECES + j) | (idx == first + 2 * N_PIECES + j)
        out = jnp.where(hit, piece, out)
    return out


def _diff_proj_kernel(h_ref, pos_row_ref, pos_col_ref, g_kv_ref, g_attn_ref, w_qt_ref, w_k_ref,
                      w_vt_ref, q_const_ref, k_const_ref, qt_ref, k_ref, vt_ref):
    h = h_ref[...]
    tm = h.shape[0]
    hr = h * _rms_scale(h)
    hk = (hr * g_kv_ref[...]).astype(BF16)
    hq = (hr * g_attn_ref[...]).astype(BF16)
    qt = _dot_nt(w_qt_ref[...], hq)
    kf = _dot(hk, w_k_ref[...])
    vt = _dot_nt(w_vt_ref[...], hk)

    row = lax.broadcasted_iota(jnp.int32, (DIFF_HEAD, tm), 0)
    q_pos = _piece_pattern(row, N_BIAS, [-p for p in _pos_pieces(pos_row_ref[...])])
    lane = lax.broadcasted_iota(jnp.int32, (tm, LANE), 1)
    k_pos = _piece_pattern(lane, DIFF_HEAD, _pos_pieces(pos_col_ref[...]))

    q_scale = DIFF_HEAD ** -0.5 * LOG2E
    for hd in range(DIFF_HEADS):
        q_bias = (q_pos + q_const_ref[hd]).astype(BF16)
        k_bias = k_pos + k_const_ref[hd]
        k12 = kf[:, hd * DIFF_V:(hd + 1) * DIFF_V]
        k_halves = (k12, pltpu.roll(k12, DIFF_HEAD, 1))
        for half in range(2):
            lo = hd * DIFF_V + half * DIFF_HEAD
            qt_ref[hd, half * LANE:half * LANE + DIFF_HEAD, :] = (
                qt[lo:lo + DIFF_HEAD] * q_scale).astype(BF16)
            qt_ref[hd, half * LANE + DIFF_HEAD:(half + 1) * LANE, :] = q_bias
            k_ref[hd, :, half * LANE:(half + 1) * LANE] = jnp.where(
                lane < DIFF_HEAD, k_halves[half], k_bias).astype(BF16)
    _store_vt(vt_ref, vt, DIFF_HEADS)


def _diff_proj(h, pos_row, pos_col, g_kv, g_attn, w_qt, w_k, w_vt, q_const, k_const):
    tm, tk = PROJ_ROW_BLOCK, ATTN_K_BLOCK
    return pl.pallas_call(
        _diff_proj_kernel,
        grid=(SEQ // tm,),
        in_specs=[
            pl.BlockSpec((tm, D_MODEL), lambda i: (i, 0)),
            pl.BlockSpec((1, tm), lambda i: (0, i)),
            pl.BlockSpec((tm, 1), lambda i: (i, 0)),
            _full((1, D_MODEL)),
            _full((1, D_MODEL)),
            _resident(w_qt.shape),
            _resident(w_k.shape),
            _resident(w_vt.shape),
            _full(q_const.shape),
            _full(k_const.shape),
        ],
        out_specs=[
            pl.BlockSpec((DIFF_HEADS, HEAD_PAD, tm), lambda i: (0, 0, i)),
            pl.BlockSpec((DIFF_HEADS, tm, HEAD_PAD), lambda i: (0, i, 0)),
            pl.BlockSpec((DIFF_HEADS, tm // tk, V_ROWS, tk), lambda i: (0, i, 0, 0)),
        ],
        out_shape=[
            jax.ShapeDtypeStruct((DIFF_HEADS, HEAD_PAD, SEQ), BF16),
            jax.ShapeDtypeStruct((DIFF_HEADS, SEQ, HEAD_PAD), BF16),
            jax.ShapeDtypeStruct((DIFF_HEADS, SEQ // tk, V_ROWS, tk), BF16),
        ],
        compiler_params=pltpu.CompilerParams(
            dimension_semantics=("arbitrary",), vmem_limit_bytes=VMEM_LIMIT),
        name="diff_proj",
    )(h, pos_row, pos_col, g_kv, g_attn, w_qt, w_k, w_vt, q_const, k_const)


def _diff_attn_kernel(lam_ref, qt_ref, k_ref, vt_ref, subln_ref, o_ref,
                      acc1_ref, acc2_ref, *s_refs):
    acc1_ref[...] = jnp.zeros(acc1_ref.shape, F32)
    acc2_ref[...] = jnp.zeros(acc2_ref.shape, F32)
    w = ATTN_STREAM_LANES
    per_half = len(s_refs) // 2
    streams = [
        _stream(qt_ref, k_ref, vt_ref, (acc1_ref, acc2_ref)[half], s_refs[n * 2 + half],
                slice(half * LANE, (half + 1) * LANE), n * w, w)
        for n in range(per_half) for half in range(2)]
    _causal_key_loop(pl.program_id(1), streams, acc1_ref.shape[1])

    lp = lam_ref[...]
    lam = (jnp.exp(jnp.sum(lp[0:1] * lp[1:2], axis=1, keepdims=True))
           - jnp.exp(jnp.sum(lp[2:3] * lp[3:4], axis=1, keepdims=True)) + LAMBDA_INIT)
    o = _normalized(acc1_ref) - lam * _normalized(acc2_ref)
    r = lax.rsqrt(jnp.mean(o * o, axis=0, keepdims=True) + EPS)
    o = (o * r).T * subln_ref[...] * (1.0 - LAMBDA_INIT)
    o_ref[...] = o.astype(o_ref.dtype)


def _diff_attn(lam, qt, k, vt, subln):
    tq, tk = DIFF_Q_BLOCK, ATTN_K_BLOCK
    acc = pltpu.VMEM((V_ROWS, tq), F32)
    s_tile = pltpu.VMEM((tk, ATTN_STREAM_LANES), F32)
    return pl.pallas_call(
        _diff_attn_kernel,
        grid=(DIFF_HEADS, SEQ // tq),
        in_specs=[
            pl.BlockSpec((4, DIFF_HEAD), lambda h, i: (0, 0)),
            pl.BlockSpec((1, HEAD_PAD, tq), lambda h, i: (h, 0, i)),
            pl.BlockSpec((1, SEQ, HEAD_PAD), lambda h, i: (h, 0, 0)),
            pl.BlockSpec((1, SEQ // tk, V_ROWS, tk), lambda h, i: (h, 0, 0, 0)),
            pl.BlockSpec((1, DIFF_V), lambda h, i: (0, 0)),
        ],
        out_specs=pl.BlockSpec((tq, DIFF_V), lambda h, i: (i, h)),
        out_shape=jax.ShapeDtypeStruct((SEQ, DIFF_HEADS * DIFF_V), BF16),
        scratch_shapes=[acc, acc] + [s_tile] * (2 * tq // ATTN_STREAM_LANES),
        compiler_params=pltpu.CompilerParams(
            dimension_semantics=("arbitrary", "arbitrary"), vmem_limit_bytes=VMEM_LIMIT),
        name="diff_attn",
    )(lam, qt, k, vt, subln)


def _alibi_constants():
    slopes = 2.0 ** (-8.0 * jnp.arange(1, DIFF_HEADS + 1, dtype=F32) / DIFF_HEADS)
    c = slopes * LOG2E
    pieces = []
    for _ in range(N_PIECES):
        piece = c.astype(BF16).astype(F32)
        pieces.append(piece)
        c = c - piece
    per_dim = jnp.repeat(jnp.stack(pieces, axis=1), N_PIECES, axis=1)
    q_const = jnp.pad(per_dim, ((0, 0), (0, DIFF_HEAD - N_BIAS)))[:, :, None]
    k_const = jnp.pad(per_dim, ((0, 0), (DIFF_HEAD + N_BIAS, LANE - DIFF_HEAD - 2 * N_BIAS)))
    return q_const, k_const[:, None, :]


def kernel(x, positions, attn_norm, ffn_norm, final_norm, mla_w_dq, mla_q_norm, mla_w_uq,
           mla_w_dkv, mla_kv_norm, mla_w_ukv, mla_w_o, diff_kv_norm, diff_w_k, diff_w_v,
           diff_w_q, diff_lambda_q1, diff_lambda_k1, diff_lambda_q2, diff_lambda_k2,
           diff_subln, diff_w_o, ffn_w_gate_up, ffn_w_down):
    x2 = x.reshape(SEQ, D_MODEL)
    pos_row = positions.reshape(1, SEQ)
    pos_col = positions.reshape(SEQ, 1)
    row_vec = lambda a: a.reshape(1, -1)

    half = QK_ROPE // 2
    inv_col = (ROPE_THETA ** (-jnp.arange(half, dtype=F32) * 2.0 / QK_ROPE)).reshape(half, 1)

    w_dkv = mla_w_dkv[0]
    w_pet = jnp.pad(w_dkv[:, KV_LORA:].T, ((0, LANE - QK_ROPE), (0, 0))).astype(BF16)
    w_ukv = mla_w_ukv[0].reshape(KV_LORA, MLA_HEADS, QK_NOPE + V_HEAD)
    w_uk = w_ukv[:, :, :QK_NOPE].reshape(KV_LORA, MLA_HEADS * QK_NOPE).astype(BF16)
    w_uvt = w_ukv[:, :, QK_NOPE:].reshape(KV_LORA, MLA_HEADS * V_HEAD).T.astype(BF16)
    qt, k, vt = _mla_proj(
        x2, pos_row.astype(F32), inv_col, row_vec(attn_norm[0]), mla_w_dq[0].astype(BF16),
        row_vec(mla_q_norm[0]), mla_w_uq[0].T.astype(BF16), w_dkv[:, :KV_LORA].astype(BF16),
        w_pet, row_vec(mla_kv_norm[0]), w_uk, w_uvt)
    o = _mla_attn(qt, k, vt)
    h = _ffn(x2, o, mla_w_o[0].astype(BF16), row_vec(ffn_norm[0]), ffn_w_gate_up[0].astype(BF16),
             ffn_w_down[0].astype(BF16), row_vec(final_norm), final_norm=False)

    q_const, k_const = _alibi_constants()
    qdt, kd, vdt = _diff_proj(h, pos_row, pos_col, row_vec(diff_kv_norm), row_vec(attn_norm[1]),
                              diff_w_q[0].T.astype(BF16), diff_w_k.astype(BF16),
                              diff_w_v.T.astype(BF16), q_const, k_const)
    lam = jnp.stack([diff_lambda_q1[0], diff_lambda_k1[0], diff_lambda_q2[0], diff_lambda_k2[0]])
    od = _diff_attn(lam, qdt, kd, vdt, row_vec(diff_subln[0]))
    out = _ffn(h, od, diff_w_o[0].astype(BF16), row_vec(ffn_norm[1]),
               ffn_w_gate_up[1].astype(BF16), ffn_w_down[1].astype(BF16), row_vec(final_norm),
               final_norm=True)
    return out.reshape(x.shape)
```
